```python
import jax, jax.numpy as jnp
from jax import lax
import numpy as np

D_MODEL = 1024
BATCH = 16
SEQ = 2048
DEPTH = 2

CHUNK = 64
Q_BLOCK = 128
HEAD_DIM = 64
N_HEADS_FOX = 6
N_HEADS_SB = 5
N_HEADS_DSA = 5
W_FOX = N_HEADS_FOX * HEAD_DIM
W_SB = N_HEADS_SB * HEAD_DIM
W_DSA = N_HEADS_DSA * HEAD_DIM
KV_LATENT = 128
N_IDX_HEADS = 4
IDX_DIM = 64
TOPK_MAX = 256
ROPE_THETA = 500000.0
ROPE_DIM = HEAD_DIM // 4
D_FF = 2816
N_BRANCH = 3
EPS = 1e-6
D_IN = 3 * W_FOX + N_HEADS_FOX + 3 * W_SB + W_DSA + KV_LATENT + N_IDX_HEADS * IDX_DIM + IDX_DIM + N_IDX_HEADS + N_BRANCH * D_MODEL

kernel_name = "hybrid_fox_stickbreak_dsa_macaron"


def _in_sizes():
    return [W_FOX, W_FOX, W_FOX, N_HEADS_FOX,
            W_SB, W_SB, W_SB,
            W_DSA, KV_LATENT, N_IDX_HEADS * IDX_DIM, IDX_DIM, N_IDX_HEADS,
            N_BRANCH * D_MODEL]


def _split_points():
    pts, acc = [], 0
    for s in _in_sizes()[:-1]:
        acc += s
        pts.append(acc)
    return pts


def rmsnorm(x, g):
    xf = x.astype(jnp.float32)
    y = xf * lax.rsqrt(jnp.mean(xf * xf, axis=-1, keepdims=True) + EPS)
    return (y * g.astype(jnp.float32)).astype(x.dtype)


def swiglu(h, w_gu, w_down):
    g, u = jnp.split(h @ w_gu, 2, axis=-1)
    return (jax.nn.silu(g) * u) @ w_down


def rope_tables(seq):
    pos = jnp.arange(seq, dtype=jnp.float32)
    inv = ROPE_THETA ** (-jnp.arange(0, ROPE_DIM, 2, dtype=jnp.float32) / ROPE_DIM)
    ang = pos[:, None] * inv[None, :]
    return jnp.cos(ang), jnp.sin(ang)


def partial_rope(x, cos, sin):
    half = ROPE_DIM // 2
    x1 = x[..., :half].astype(jnp.float32)
    x2 = x[..., half:ROPE_DIM].astype(jnp.float32)
    c = cos[None, :, None, :]
    s = sin[None, :, None, :]
    r1 = (x1 * c - x2 * s).astype(x.dtype)
    r2 = (x2 * c + x1 * s).astype(x.dtype)
    return jnp.concatenate([r1, r2, x[..., ROPE_DIM:]], axis=-1)


def forgetting_attention(q, k, v, f_logit):
    B, S, H, Dh = q.shape
    scale = Dh ** -0.5
    c = jnp.cumsum(jax.nn.log_sigmoid(f_logit.astype(jnp.float32)), axis=1)
    c = c.transpose(0, 2, 1)
    outs = []
    for start in range(0, S, Q_BLOCK):
        end = start + Q_BLOCK
        s = jnp.einsum('bqhd,bkhd->bhqk', q[:, start:end], k[:, :end]).astype(jnp.float32) * scale
        bias = c[:, :, start:end, None] - c[:, :, None, :end]
        t_pos = jnp.arange(start, end)
        s_pos = jnp.arange(end)
        mask = s_pos[None, :] <= t_pos[:, None]
        p = jax.nn.softmax(jnp.where(mask, s + bias, -jnp.inf), axis=-1)
        outs.append(jnp.einsum('bhqk,bkhd->bqhd', p.astype(v.dtype), v[:, :end]))
    return jnp.concatenate(outs, axis=1)


def stick_breaking_attention(q, k, v):
    B, S, H, Dh = q.shape
    scale = Dh ** -0.5
    outs = []
    for start in range(0, S, Q_BLOCK):
        end = start + Q_BLOCK
        z = jnp.einsum('bqhd,bkhd->bhqk', q[:, start:end], k[:, :end]).astype(jnp.float32) * scale
        t_pos = jnp.arange(start, end)
        s_pos = jnp.arange(end)
        mask = s_pos[None, :] < t_pos[:, None]
        log_1m = jnp.where(mask, jax.nn.log_sigmoid(-z), 0.0)
        acc = lax.cumsum(log_1m, axis=3, reverse=True) - log_1m
        a = jnp.where(mask, jnp.exp(jax.nn.log_sigmoid(z) + acc), 0.0)
        outs.append(jnp.einsum('bhqk,bkhd->bqhd', a.astype(v.dtype), v[:, :end]))
    return jnp.concatenate(outs, axis=1)


def dsa_attention(q, k, v, iq, ik, iw):
    B, S, H, Dh = q.shape
    scale = Dh ** -0.5
    chunk_id = jnp.arange(S) // CHUNK
    topk = min(TOPK_MAX, S // 4)
    outs = []
    for start in range(0, S, Q_BLOCK):
        end = start + Q_BLOCK
        kk = min(topk, end)
        q_chunk = chunk_id[start:end]
        rel = jax.nn.relu(jnp.einsum('bqhd,bkd->bqhk', iq[:, start:end], ik[:, :end]).astype(jnp.float32))
        score = jnp.einsum('bqhk,bqh->bqk', rel, iw[:, start:end].astype(jnp.float32))
        admissible = chunk_id[None, :end] <= q_chunk[:, None]
        score = jnp.where(admissible[None], score, -jnp.inf)
        _, sel = lax.top_k(score, kk)
        valid = chunk_id[sel] <= q_chunk[None, :, None]
        k_sel = jax.vmap(lambda kb, ib: kb[ib])(k[:, :end], sel)
        v_sel = jax.vmap(lambda vb, ib: vb[ib])(v[:, :end], sel)
        s = jnp.einsum('bqhd,bqkd->bqhk', q[:, start:end], k_sel).astype(jnp.float32) * scale
        p = jax.nn.softmax(jnp.where(valid[:, :, None, :], s, -jnp.inf), axis=-1)
        outs.append(jnp.einsum('bqhk,bqkd->bqhd', p.astype(v.dtype), v_sel))
    return jnp.concatenate(outs, axis=1)


def hybrid_mixer(h, w_in, b_forget, g_kv, w_kv_up, g_idx_k, w_up_fox, w_up_sb, w_up_dsa, w_out, cos, sin):
    B, S, _ = h.shape
    (q_a, k_a, v_a, f_a, q_b, k_b, v_b, q_c, c_kv, q_i, k_i, w_i, gates) = jnp.split(h @ w_in, _split_points(), axis=-1)
    heads = lambda t, n: t.reshape(B, S, n, HEAD_DIM)
    y_a = forgetting_attention(heads(q_a, N_HEADS_FOX), heads(k_a, N_HEADS_FOX), heads(v_a, N_HEADS_FOX), f_a + b_forget)
    y_b = stick_breaking_attention(heads(q_b, N_HEADS_SB), heads(k_b, N_HEADS_SB), heads(v_b, N_HEADS_SB))
    k_c, v_c = jnp.split(rmsnorm(c_kv, g_kv) @ w_kv_up, 2, axis=-1)
    q_c = partial_rope(heads(q_c, N_HEADS_DSA), cos, sin)
    k_c = partial_rope(k_c[:, :, None, :], cos, sin)[:, :, 0]
    q_i = partial_rope(q_i.reshape(B, S, N_IDX_HEADS, IDX_DIM), cos, sin)
    k_i = partial_rope(rmsnorm(k_i, g_idx_k)[:, :, None, :], cos, sin)[:, :, 0]
    y_c = dsa_attention(q_c, k_c, v_c, q_i, k_i, w_i)
    g_a, g_b, g_c = jnp.split(jax.nn.sigmoid(gates), N_BRANCH, axis=-1)
    merged = (g_a * (y_a.reshape(B, S, W_FOX) @ w_up_fox)
              + g_b * (y_b.reshape(B, S, W_SB) @ w_up_sb)
              + g_c * (y_c.reshape(B, S, W_DSA) @ w_up_dsa))
    return merged @ w_out


def setup_inputs(seed: int = 0) -> dict:
    key = jax.random.key(seed)
    ks = jax.random.split(key, 20)
    f32 = jnp.float32
    nrm = lambda k, shape, fan_in: jax.random.normal(k, shape, f32) * (fan_in ** -0.5)
    gain = lambda k, shape: 1.0 + 0.02 * jax.random.normal(k, shape, f32)
    return {
        "x": jax.random.normal(ks[0], (BATCH, SEQ, D_MODEL), f32),
        "g_ffn1": gain(ks[1], (DEPTH, D_MODEL)),
        "w_ffn1_gu": nrm(ks[2], (DEPTH, D_MODEL, 2 * D_FF), D_MODEL),
        "w_ffn1_down": nrm(ks[3], (DEPTH, D_FF, D_MODEL), D_FF),
        "g_mix": gain(ks[4], (DEPTH, D_MODEL)),
        "w_in": nrm(ks[5], (DEPTH, D_MODEL, D_IN), D_MODEL),
        "b_forget": 2.0 + 0.5 * jax.random.normal(ks[6], (DEPTH, N_HEADS_FOX), f32),
        "g_kv_latent": gain(ks[7], (DEPTH, KV_LATENT)),
        "w_kv_up": nrm(ks[8], (DEPTH, KV_LATENT, 2 * HEAD_DIM), KV_LATENT),
        "g_idx_k": gain(ks[9], (DEPTH, IDX_DIM)),
        "w_up_fox": nrm(ks[10], (DEPTH, W_FOX, D_MODEL), W_FOX),
        "w_up_sb": nrm(ks[11], (DEPTH, W_SB, D_MODEL), W_SB),
        "w_up_dsa": nrm(ks[12], (DEPTH, W_DSA, D_MODEL), W_DSA),
        "w_out": nrm(ks[13], (DEPTH, D_MODEL, D_MODEL), D_MODEL),
        "g_ffn2": gain(ks[14], (DEPTH, D_MODEL)),
        "w_ffn2_gu": nrm(ks[15], (DEPTH, D_MODEL, 2 * D_FF), D_MODEL),
        "w_ffn2_down": nrm(ks[16], (DEPTH, D_FF, D_MODEL), D_FF),
        "g_final": gain(ks[17], (D_MODEL,)),
    }


def reference(x, g_ffn1, w_ffn1_gu, w_ffn1_down, g_mix, w_in, b_forget, g_kv_latent, w_kv_up, g_idx_k,
              w_up_fox, w_up_sb, w_up_dsa, w_out, g_ffn2, w_ffn2_gu, w_ffn2_down, g_final):
    cos, sin = rope_tables(x.shape[1])
    for l in range(DEPTH):
        x = x + 0.5 * swiglu(rmsnorm(x, g_ffn1[l]), w_ffn1_gu[l], w_ffn1_down[l])
        x = x + hybrid_mixer(rmsnorm(x, g_mix[l]), w_in[l], b_forget[l], g_kv_latent[l], w_kv_up[l], g_idx_k[l],
                             w_up_fox[l], w_up_sb[l], w_up_dsa[l], w_out[l], cos, sin)
        x = x + 0.5 * swiglu(rmsnorm(x, g_ffn2[l]), w_ffn2_gu[l], w_ffn2_down[l])
    return rmsnorm(x, g_final)
```

```python
import functools

import jax
import jax.numpy as jnp
from jax import lax
from jax.experimental import pallas as pl
from jax.experimental.pallas import tpu as pltpu

F32 = jnp.float32
BF16 = jnp.bfloat16
I32 = jnp.int32

D_MODEL = 1024
HEAD_DIM = 64
N_HEADS_FOX = 6
N_HEADS_SB = 5
N_HEADS_DSA = 5
W_FOX = N_HEADS_FOX * HEAD_DIM
W_SB = N_HEADS_SB * HEAD_DIM
W_DSA = N_HEADS_DSA * HEAD_DIM
KV_LATENT = 128
N_IDX_HEADS = 4
IDX_DIM = 64
CHUNK = 64
TOPK_MAX = 256
ROPE_THETA = 500000.0
ROPE_DIM = HEAD_DIM // 4
D_FF = 2816
EPS = 1e-6

LANES = 128
PAIR_W = 2 * HEAD_DIM
W_PAIRS = 3 * PAIR_W
VMEM_LIMIT = 56 * 1024 * 1024

GA_W = 3 * W_FOX
GB_W = 3 * W_PAIRS
GC_ROPE_W = W_DSA + N_IDX_HEADS * IDX_DIM + IDX_DIM
GC_W = GC_ROPE_W + KV_LATENT
GS_W = LANES
GG_W = 3 * D_MODEL
GA_0 = 0
GB_0 = GA_0 + GA_W
GC_0 = GB_0 + GB_W
GS_0 = GC_0 + GC_W
GG_0 = GS_0 + GS_W
D_IN_PAD = GG_0 + GG_W

FF_CHUNK = 256
N_FF_CHUNKS = D_FF // FF_CHUNK

TM_FFN = 512
TM_PROJ = 256
TM_MERGE = 512
TQ = 256
QB_DSA = 128
KB_DSA = 256
NEG_BIG = -1e30
INT_MIN = -2147483648


def _cparams(sem):
    return pltpu.CompilerParams(dimension_semantics=sem, vmem_limit_bytes=VMEM_LIMIT)


def _resident(shape):
    nd = len(shape)
    return pl.BlockSpec(shape, lambda *_: (0,) * nd, pipeline_mode=pl.Buffered(1))


def _rms(x, g):
    ms = jnp.mean(x * x, axis=-1, keepdims=True)
    return x * lax.rsqrt(ms + EPS) * g


def _dot(a, b):
    return jnp.dot(a, b, preferred_element_type=F32)


def _dot_nt(a, b):
    return lax.dot_general(a, b, (((1,), (1,)), ((), ())), preferred_element_type=F32)


def _log_sigmoid(z):
    return jnp.minimum(z, 0.0) - jnp.log1p(jnp.exp(-jnp.abs(z)))


def _ffn_kernel(x_ref, g_ref, w1_ref, w2_ref, gf_ref, o_ref, acc_ref, *, final_norm):
    x = x_ref[...]
    h = _rms(x, g_ref[...]).astype(BF16)
    for j in range(N_FF_CHUNKS):
        gu = _dot(h, w1_ref[j])
        g = gu[:, :FF_CHUNK]
        u = gu[:, FF_CHUNK:]
        a = (g * jax.nn.sigmoid(g) * u).astype(BF16)
        d = _dot(a, w2_ref[j])
        if j == 0:
            acc_ref[...] = d
        else:
            acc_ref[...] += d
    y = x + 0.5 * acc_ref[...]
    if final_norm:
        y = _rms(y, gf_ref[...])
    o_ref[...] = y


def _ffn(x, g, w1, w2, gf, final_norm):
    m = x.shape[0]
    return pl.pallas_call(
        functools.partial(_ffn_kernel, final_norm=final_norm),
        grid=(m // TM_FFN,),
        in_specs=[
            pl.BlockSpec((TM_FFN, D_MODEL), lambda i: (i, 0)),
            _resident((1, D_MODEL)),
            _resident(w1.shape),
            _resident(w2.shape),
            _resident((1, D_MODEL)),
        ],
        out_specs=pl.BlockSpec((TM_FFN, D_MODEL), lambda i: (i, 0)),
        out_shape=jax.ShapeDtypeStruct((m, D_MODEL), F32),
        scratch_shapes=[pltpu.VMEM((TM_FFN, D_MODEL), F32)],
        compiler_params=_cparams(("arbitrary",)),
        name="ffn",
    )(x, g, w1, w2, gf)


def _rope_tile(t, c, s1, s2):
    return t * c + pltpu.roll(t, LANES - ROPE_DIM // 2, 1) * s1 + pltpu.roll(t, ROPE_DIM // 2, 1) * s2


def _proj_kernel(x_ref, g_ref, w_ref, wkv_ref, gkv_ref, gki_ref, rc_ref, rs1_ref, rs2_ref,
                 qa_ref, qb_ref, dq_ref, dkk_ref, vt_ref, sm_ref, gt_ref):
    h = _rms(x_ref[...], g_ref[...]).astype(BF16)

    def proj(c0, width):
        return _dot(h, w_ref[:, c0:c0 + width])

    for c0 in range(0, GA_W, 384):
        qa_ref[:, c0:c0 + 384] = proj(GA_0 + c0, 384).astype(BF16)
    for c0 in range(0, GB_W, 384):
        qb_ref[:, c0:c0 + 384] = proj(GB_0 + c0, 384).astype(BF16)
    sm_ref[...] = proj(GS_0, GS_W)
    for c0 in range(0, GG_W, 512):
        gt_ref[:, c0:c0 + 512] = proj(GG_0 + c0, 512)

    r = proj(GC_0, GC_W)
    lane = lax.broadcasted_iota(I32, (1, LANES), 1)
    hi_half = lane >= HEAD_DIM
    rc, rs1, rs2 = rc_ref[...], rs1_ref[...], rs2_ref[...]
    n_tiles = GC_ROPE_W // LANES
    last = None
    for k in range(n_tiles):
        t = r[:, k * LANES:(k + 1) * LANES]
        if k == n_tiles - 1:
            ms = jnp.sum(jnp.where(hi_half, t * t, 0.0), axis=-1, keepdims=True) * (1.0 / IDX_DIM)
            t = t * jnp.where(hi_half, lax.rsqrt(ms + EPS) * gki_ref[...], 1.0)
        t = _rope_tile(t, rc, rs1, rs2)
        dq_ref[:, k * LANES:(k + 1) * LANES] = t.astype(BF16)
        last = t
    ckv = _rms(r[:, GC_ROPE_W:GC_W], gkv_ref[...]).astype(BF16)
    kv = _dot(ckv, wkv_ref[...])
    lo_half = jnp.logical_not(hi_half)
    kv = _rope_tile(kv, jnp.where(lo_half, rc, 1.0), jnp.where(lo_half, rs1, 0.0),
                    jnp.where(lo_half, rs2, 0.0))
    dkk_ref[...] = jnp.where(lo_half, kv, last).astype(BF16)
    vt_ref[...] = kv.T[HEAD_DIM:, :].astype(BF16)


def _proj(x, g, w, wkv, gkv, gki, rc, rs1, rs2, batch, seq):
    m = x.shape[0]
    spb = seq // TM_PROJ
    row = lambda width: pl.BlockSpec((TM_PROJ, width), lambda i: (i, 0))
    pos = pl.BlockSpec((TM_PROJ, LANES), lambda i: (i % spb, 0))
    return pl.pallas_call(
        _proj_kernel,
        grid=(m // TM_PROJ,),
        in_specs=[
            row(D_MODEL), _resident((1, D_MODEL)), _resident(w.shape), _resident(wkv.shape),
            _resident((1, KV_LATENT)), _resident((1, LANES)), pos, pos, pos,
        ],
        out_specs=[
            row(GA_W), row(GB_W), row(GC_ROPE_W), row(LANES),
            pl.BlockSpec((None, HEAD_DIM, TM_PROJ), lambda i: (i // spb, 0, i % spb)),
            row(GS_W), row(GG_W),
        ],
        out_shape=[
            jax.ShapeDtypeStruct((m, GA_W), BF16),
            jax.ShapeDtypeStruct((m, GB_W), BF16),
            jax.ShapeDtypeStruct((m, GC_ROPE_W), BF16),
            jax.ShapeDtypeStruct((m, LANES), BF16),
            jax.ShapeDtypeStruct((batch, HEAD_DIM, seq), BF16),
            jax.ShapeDtypeStruct((m, GS_W), F32),
            jax.ShapeDtypeStruct((m, GG_W), F32),
        ],
        compiler_params=_cparams(("arbitrary",)),
        name="in_proj",
    )(x, g, w, wkv, gkv, gki, rc, rs1, rs2)


def _gate_kernel(s_ref, b_ref, c_ref, ct_ref, wt_ref):
    xt = s_ref[...].T
    c = _log_sigmoid(xt + b_ref[...])
    seq = c.shape[1]
    lane = lax.broadcasted_iota(I32, c.shape, 1)
    d = 1
    while d < seq:
        c = c + jnp.where(lane >= d, pltpu.roll(c, d, 1), 0.0)
        d *= 2
    ct_ref[...] = c[0:8]
    wt_ref[...] = xt[8:16]
    c_ref[...] = c.T


def _gate(small, bcol, batch, seq):
    return pl.pallas_call(
        _gate_kernel,
        grid=(batch,),
        in_specs=[pl.BlockSpec((seq, LANES), lambda b: (b, 0)), _resident((LANES, 1))],
        out_specs=[
            pl.BlockSpec((seq, LANES), lambda b: (b, 0)),
            pl.BlockSpec((None, 8, seq), lambda b: (b, 0, 0)),
            pl.BlockSpec((None, 8, seq), lambda b: (b, 0, 0)),
        ],
        out_shape=[
            jax.ShapeDtypeStruct((batch * seq, LANES), F32),
            jax.ShapeDtypeStruct((batch, 8, seq), F32),
            jax.ShapeDtypeStruct((batch, 8, seq), F32),
        ],
        compiler_params=_cparams(("arbitrary",)),
        name="gate_cumsum",
    )(small, bcol)


def _fox_kernel(q_ref, k_ref, v_ref, c_ref, ct_ref, o_ref):
    pair = pl.program_id(1)
    i = pl.program_id(2)
    q = q_ref[...]
    cblk = c_ref[...]
    lane = lax.broadcasted_iota(I32, (1, LANES), 1)
    sub8 = lax.broadcasted_iota(I32, (8, 1), 0)
    causal = (lax.broadcasted_iota(I32, (TQ, TQ), 1) <= lax.broadcasted_iota(I32, (TQ, TQ), 0))
    outs = []
    for e in range(2):
        hidx = 2 * pair + e
        half = (lane < HEAD_DIM) if e == 0 else (lane >= HEAD_DIM)
        qm = jnp.where(half, q, jnp.zeros_like(q))
        cq = jnp.sum(jnp.where(lane == hidx, cblk, 0.0), axis=1, keepdims=True)

        def block(j, carry, masked):
            m, l, acc = carry
            r0 = pl.multiple_of(j * TQ, TQ)
            kb = k_ref[pl.ds(r0, TQ), :]
            vb = v_ref[pl.ds(r0, TQ), :]
            ck = jnp.sum(jnp.where(sub8 == hidx, ct_ref[:, pl.ds(r0, TQ)], 0.0), axis=0, keepdims=True)
            u = _dot_nt(qm, kb) - ck
            if masked:
                u = jnp.where(causal, u, -jnp.inf)
            m_new = jnp.maximum(m, jnp.max(u, axis=1, keepdims=True) + cq)
            alpha = jnp.exp(m - m_new)
            p = jnp.exp(u + (cq - m_new))
            l = alpha * l + jnp.sum(p, axis=1, keepdims=True)
            acc = alpha * acc + _dot(p.astype(BF16), vb)
            return m_new, l, acc

        init = (jnp.full((TQ, 1), -jnp.inf, F32), jnp.zeros((TQ, 1), F32), jnp.zeros((TQ, LANES), F32))
        carry = lax.fori_loop(0, i, functools.partial(block, masked=False), init)
        _, l, acc = block(i, carry, True)
        outs.append(acc * (1.0 / l))
    o_ref[...] = jnp.where(lane < HEAD_DIM, outs[0], outs[1]).astype(BF16)


def _fox(qkv, c, ct, batch, seq):
    nq = seq // TQ
    n_pairs = N_HEADS_FOX // 2
    return pl.pallas_call(
        _fox_kernel,
        grid=(batch, n_pairs, nq),
        in_specs=[
            pl.BlockSpec((TQ, LANES), lambda b, p, i: (b * nq + i, p)),
            pl.BlockSpec((seq, LANES), lambda b, p, i: (b, n_pairs + p)),
            pl.BlockSpec((seq, LANES), lambda b, p, i: (b, 2 * n_pairs + p)),
            pl.BlockSpec((TQ, LANES), lambda b, p, i: (b * nq + i, 0)),
            pl.BlockSpec((None, 8, seq), lambda b, p, i: (b, 0, 0)),
        ],
        out_specs=pl.BlockSpec((TQ, LANES), lambda b, p, i: (b * nq + i, p)),
        out_shape=jax.ShapeDtypeStruct((batch * seq, W_PAIRS), BF16),
        compiler_params=_cparams(("arbitrary", "arbitrary", "arbitrary")),
        name="fox_attn",
    )(qkv, qkv, qkv, c, ct)


def _sb_kernel(q_ref, k_ref, v_ref, o_ref):
    pair = pl.program_id(1)
    i = pl.program_id(2)
    q = q_ref[...]
    lane = lax.broadcasted_iota(I32, (1, LANES), 1)
    row = lax.broadcasted_iota(I32, (TQ, TQ), 0)
    col = lax.broadcasted_iota(I32, (TQ, TQ), 1)
    strict = col < row
    later = (row > col).astype(BF16)

    def head(e):
        half = (lane < HEAD_DIM) if e == 0 else (lane >= HEAD_DIM)
        qm = jnp.where(half, q, jnp.zeros_like(q))

        def block(j, carry, masked):
            tail, acc = carry
            r0 = pl.multiple_of(j * TQ, TQ)
            z = _dot_nt(qm, k_ref[pl.ds(r0, TQ), :])
            l1m = -(jnp.maximum(z, 0.0) + jnp.log1p(jnp.exp(-jnp.abs(z))))
            if masked:
                l1m = jnp.where(strict, l1m, 0.0)
            hi = l1m.astype(BF16)
            lo = (l1m - hi.astype(F32)).astype(BF16)
            within = _dot(hi, later) + _dot(lo, later)
            a = jnp.exp(z + l1m + within + tail)
            if masked:
                a = jnp.where(strict, a, 0.0)
            acc = acc + _dot(a.astype(BF16), v_ref[pl.ds(r0, TQ), :])
            tail = tail + jnp.sum(l1m, axis=1, keepdims=True)
            return tail, acc

        carry = block(i, (jnp.zeros((TQ, 1), F32), jnp.zeros((TQ, LANES), F32)), True)
        _, acc = lax.fori_loop(0, i, lambda jj, c: block(i - 1 - jj, c, False), carry)
        return acc

    o_ref[...] = jnp.where(lane < HEAD_DIM, head(0), 0.0).astype(BF16)

    @pl.when(pair < N_HEADS_SB // 2)
    def _():
        o_ref[...] = jnp.where(lane < HEAD_DIM, o_ref[...], head(1).astype(BF16))


def _sb(qkv, batch, seq):
    nq = seq // TQ
    n_pairs = W_PAIRS // PAIR_W
    return pl.pallas_call(
        _sb_kernel,
        grid=(batch, n_pairs, nq),
        in_specs=[
            pl.BlockSpec((TQ, LANES), lambda b, p, i: (b * nq + i, p)),
            pl.BlockSpec((seq, LANES), lambda b, p, i: (b, n_pairs + p)),
            pl.BlockSpec((seq, LANES), lambda b, p, i: (b, 2 * n_pairs + p)),
        ],
        out_specs=pl.BlockSpec((TQ, LANES), lambda b, p, i: (b * nq + i, p)),
        out_shape=jax.ShapeDtypeStruct((batch * seq, W_PAIRS), BF16),
        compiler_params=_cparams(("arbitrary", "arbitrary", "arbitrary")),
        name="sb_attn",
    )(qkv, qkv, qkv)


def _dsa_kernel(q_ref, kk_ref, vt_ref, wt_ref, y_ref, key_ref):
    i = pl.program_id(1)
    n_kb = i // 2 + 1
    k_sel = jnp.minimum(TOPK_MAX, QB_DSA * (i + 1))
    lane = lax.broadcasted_iota(I32, (1, LANES), 1)
    lo_half = lane < HEAD_DIM

    qb = q_ref[...].astype(F32)
    tiles = [qb[:, k * LANES:(k + 1) * LANES] for k in range(GC_ROPE_W // LANES)]
    swap = lambda t: pltpu.roll(t, HEAD_DIM, 1)
    keep_lo = lambda t: jnp.where(lo_half, t, 0.0)
    keep_hi = lambda t: jnp.where(lo_half, 0.0, t)
    r_att = jnp.concatenate(
        [keep_lo(tiles[0]), keep_lo(swap(tiles[0])), keep_lo(tiles[1]), keep_lo(swap(tiles[1])),
         keep_lo(tiles[2])], axis=0).astype(BF16)
    r_idx = jnp.concatenate(
        [keep_hi(tiles[2]), keep_hi(swap(tiles[3])), keep_hi(tiles[3]), keep_hi(swap(tiles[4]))],
        axis=0).astype(BF16)
    wt = wt_ref[...]
    q_chunk = (i * QB_DSA + lane) // CHUNK
    k_chunk0 = lax.broadcasted_iota(I32, (KB_DSA, 1), 0) // CHUNK

    def kslice(jb):
        return pl.ds(pl.multiple_of(jb * KB_DSA, KB_DSA), KB_DSA)

    def score_block(jb, _):
        raw = _dot_nt(kk_ref[kslice(jb), :], r_idx)
        sc = jnp.zeros((KB_DSA, LANES), F32)
        for h in range(N_IDX_HEADS):
            sc = sc + jnp.maximum(raw[:, h * LANES:(h + 1) * LANES], 0.0) * wt[h:h + 1, :]
        adm = (k_chunk0 + jb * (KB_DSA // CHUNK)) <= q_chunk
        sc = jnp.where(adm, sc, -jnp.inf)
        bits = lax.bitcast_convert_type(sc, I32)
        key = jnp.where(bits < 0, bits ^ 0x7FFFFFFF, bits)
        key_ref[kslice(jb), :] = jnp.where(sc == 0.0, 0, key)
        return 0

    lax.fori_loop(0, n_kb, score_block, 0)

    def count_ge(cand):
        def body(jb, acc):
            ge = (key_ref[kslice(jb), :] >= cand).astype(I32)
            return acc + jnp.sum(ge.reshape(KB_DSA // 8, 8, LANES), axis=0)
        acc = lax.fori_loop(0, n_kb, body, jnp.zeros((8, LANES), I32))
        return jnp.sum(acc, axis=0, keepdims=True)

    def bisect(it, lo):
        cand = lo + jnp.left_shift(jnp.int32(1), 31 - it)
        return jnp.where(count_ge(cand) >= k_sel, cand, lo)

    thr = lax.fori_loop(0, 32, bisect, jnp.full((1, LANES), INT_MIN, I32))
    need = (k_sel - count_ge(thr + 1)).astype(F32)
    neg_inf_key = jnp.int32(-8388608) ^ 0x7FFFFFFF
    before = (lax.broadcasted_iota(I32, (KB_DSA, KB_DSA), 1)
              < lax.broadcasted_iota(I32, (KB_DSA, KB_DSA), 0)).astype(BF16)

    def attend(jb, carry):
        m, l, acc, ties = carry
        keyb = key_ref[kslice(jb), :]
        eq = keyb == thr
        eqf = jnp.where(eq, 1.0, 0.0)
        rank = _dot(before, eqf.astype(BF16)) + ties
        sel = ((keyb > thr) | (eq & (rank < need))) & (keyb > neg_inf_key)
        bias = jnp.where(sel, 0.0, NEG_BIG)
        ties = ties + jnp.sum(eqf, axis=0, keepdims=True)
        st = _dot_nt(kk_ref[kslice(jb), :], r_att)
        st = st + jnp.concatenate([bias] * N_HEADS_DSA, axis=1)
        m_new = jnp.maximum(m, jnp.max(st, axis=0, keepdims=True))
        alpha = jnp.exp(m - m_new)
        p = jnp.exp(st - m_new)
        l = alpha * l + jnp.sum(p, axis=0, keepdims=True)
        acc = alpha * acc + _dot(vt_ref[:, kslice(jb)], p.astype(BF16))
        return m_new, l, acc, ties

    wide = N_HEADS_DSA * LANES
    init = (jnp.full((1, wide), NEG_BIG, F32), jnp.zeros((1, wide), F32),
            jnp.zeros((HEAD_DIM, wide), F32), jnp.zeros((1, LANES), F32))
    _, l, acc, _ = lax.fori_loop(0, n_kb, attend, init)
    o = acc * (1.0 / l)
    heads = [o[:, h * LANES:(h + 1) * LANES] for h in range(N_HEADS_DSA)]
    heads.append(jnp.zeros_like(heads[0]))
    for g in range(W_PAIRS // PAIR_W):
        y_ref[:, g * LANES:(g + 1) * LANES] = jnp.concatenate(
            [heads[2 * g], heads[2 * g + 1]], axis=0).T.astype(BF16)


def _dsa(dq, dkk, vt, wt, batch, seq):
    nq = seq // QB_DSA
    return pl.pallas_call(
        _dsa_kernel,
        grid=(batch, nq),
        in_specs=[
            pl.BlockSpec((QB_DSA, GC_ROPE_W), lambda b, i: (b * nq + i, 0)),
            pl.BlockSpec((seq, LANES), lambda b, i: (b, 0)),
            pl.BlockSpec((None, HEAD_DIM, seq), lambda b, i: (b, 0, 0)),
            pl.BlockSpec((None, 8, QB_DSA), lambda b, i: (b, 0, i)),
        ],
        out_specs=pl.BlockSpec((QB_DSA, W_PAIRS), lambda b, i: (b * nq + i, 0)),
        out_shape=jax.ShapeDtypeStruct((batch * seq, W_PAIRS), BF16),
        scratch_shapes=[pltpu.VMEM((seq, LANES), I32)],
        compiler_params=_cparams(("arbitrary", "arbitrary")),
        name="dsa_attn",
    )(dq, dkk, vt, wt)


def _merge_kernel(x_ref, ya_ref, yb_ref, yc_ref, gt_ref, wa_ref, wb_ref, wc_ref, wo_ref, o_ref):
    merged = None
    for k, (y_ref, w_ref) in enumerate(((ya_ref, wa_ref), (yb_ref, wb_ref), (yc_ref, wc_ref))):
        gate = jax.nn.sigmoid(gt_ref[:, k * D_MODEL:(k + 1) * D_MODEL])
        term = gate * _dot(y_ref[...], w_ref[...])
        merged = term if merged is None else merged + term
    o_ref[...] = x_ref[...] + _dot(merged.astype(BF16), wo_ref[...])


def _merge(x, ya, yb, yc, gates, wa, wb, wc, wo):
    m = x.shape[0]
    row = lambda width: pl.BlockSpec((TM_MERGE, width), lambda i: (i, 0))
    return pl.pallas_call(
        _merge_kernel,
        grid=(m // TM_MERGE,),
        in_specs=[row(D_MODEL), row(W_PAIRS), row(W_PAIRS), row(W_PAIRS), row(GG_W),
                  _resident(wa.shape), _resident(wb.shape), _resident(wc.shape), _resident(wo.shape)],
        out_specs=row(D_MODEL),
        out_shape=jax.ShapeDtypeStruct((m, D_MODEL), F32),
        compiler_params=_cparams(("arbitrary",)),
        name="merge_out",
    )(x, ya, yb, yc, gates, wa, wb, wc, wo)


def _pad_heads(w, axis):
    pad = [(0, 0)] * w.ndim
    pad[axis] = (0, W_PAIRS - w.shape[axis])
    return jnp.pad(w, pad)


def _layout_w_in(w_in):
    sizes = [W_FOX, W_FOX, W_FOX, N_HEADS_FOX, W_SB, W_SB, W_SB, W_DSA, KV_LATENT,
             N_IDX_HEADS * IDX_DIM, IDX_DIM, N_IDX_HEADS, 3 * D_MODEL]
    pieces, acc = [], 0
    for s in sizes:
        pieces.append(w_in[:, acc:acc + s])
        acc += s
    q_a, k_a, v_a, f_a, q_b, k_b, v_b, q_c, c_kv, q_i, k_i, w_i, gates = pieces
    scale = HEAD_DIM ** -0.5
    zeros = lambda n: jnp.zeros((D_MODEL, n), w_in.dtype)
    cols = [q_a * scale, k_a, v_a,
            _pad_heads(q_b * scale, 1), _pad_heads(k_b, 1), _pad_heads(v_b, 1),
            q_c * scale, q_i, k_i, c_kv,
            f_a, zeros(8 - N_HEADS_FOX), w_i, zeros(GS_W - 8 - N_IDX_HEADS),
            gates]
    out = jnp.concatenate(cols, axis=1).astype(BF16)
    assert out.shape == (D_MODEL, D_IN_PAD)
    return out


def _layout_ffn(w_gu, w_down):
    wg = w_gu[:, :D_FF].reshape(D_MODEL, N_FF_CHUNKS, FF_CHUNK)
    wu = w_gu[:, D_FF:].reshape(D_MODEL, N_FF_CHUNKS, FF_CHUNK)
    w1 = jnp.concatenate([wg, wu], axis=2).transpose(1, 0, 2).astype(BF16)
    w2 = w_down.reshape(N_FF_CHUNKS, FF_CHUNK, D_MODEL).astype(BF16)
    return w1, w2


def _rope_tables(seq):
    half = ROPE_DIM // 2
    pos = jnp.arange(seq, dtype=F32)
    inv = ROPE_THETA ** (-jnp.arange(0, ROPE_DIM, 2, dtype=F32) / ROPE_DIM)
    ang = pos[:, None] * inv[None, :]
    cos, sin = jnp.cos(ang), jnp.sin(ang)
    ones = jnp.ones((seq, HEAD_DIM - ROPE_DIM), F32)
    zeros_rest = jnp.zeros((seq, HEAD_DIM - ROPE_DIM), F32)
    zeros_half = jnp.zeros((seq, half), F32)
    c = jnp.concatenate([cos, cos, ones], axis=1)
    s1 = jnp.concatenate([-sin, zeros_half, zeros_rest], axis=1)
    s2 = jnp.concatenate([zeros_half, sin, zeros_rest], axis=1)
    two = lambda t: jnp.concatenate([t, t], axis=1)
    return two(c), two(s1), two(s2)


@jax.jit
def kernel(x, g_ffn1, w_ffn1_gu, w_ffn1_down, g_mix, w_in, b_forget, g_kv_latent, w_kv_up, g_idx_k,
           w_up_fox, w_up_sb, w_up_dsa, w_out, g_ffn2, w_ffn2_gu, w_ffn2_down, g_final):
    batch, seq, _ = x.shape
    depth = g_ffn1.shape[0]
    rc, rs1, rs2 = _rope_tables(seq)
    xf = x.reshape(batch * seq, D_MODEL)
    row = lambda v: v.reshape(1, -1).astype(F32)
    gf = row(g_final)
    for layer in range(depth):
        w1, w2 = _layout_ffn(w_ffn1_gu[layer], w_ffn1_down[layer])
        xf = _ffn(xf, row(g_ffn1[layer]), w1, w2, gf, False)

        gki = jnp.concatenate([jnp.ones((IDX_DIM,), F32), g_idx_k[layer]]).reshape(1, LANES)
        qkv_a, qkv_b, dq, dkk, vt, small, gates = _proj(
            xf, row(g_mix[layer]), _layout_w_in(w_in[layer]), w_kv_up[layer].astype(BF16),
            row(g_kv_latent[layer]), gki, rc, rs1, rs2, batch, seq)
        bcol = jnp.zeros((LANES, 1), F32).at[:N_HEADS_FOX, 0].set(b_forget[layer])
        c, ct, wt = _gate(small, bcol, batch, seq)
        y_a = _fox(qkv_a, c, ct, batch, seq)
        y_b = _sb(qkv_b, batch, seq)
        y_c = _dsa(dq, dkk, vt, wt, batch, seq)
        xf = _merge(xf, y_a, y_b, y_c, gates,
                    w_up_fox[layer].astype(BF16), _pad_heads(w_up_sb[layer], 0).astype(BF16),
                    _pad_heads(w_up_dsa[layer], 0).astype(BF16), w_out[layer].astype(BF16))

        w1, w2 = _layout_ffn(w_ffn2_gu[layer], w_ffn2_down[layer])
        xf = _ffn(xf, row(g_ffn2[layer]), w1, w2, gf, layer == depth - 1)
    return xf.reshape(batch, seq, D_MODEL)
```

```python
import functools

import jax
import jax.numpy as jnp
from jax import lax
from jax.experimental import pallas as pl
from jax.experimental.pallas import tpu as pltpu

F32 = jnp.float32
BF16 = jnp.bfloat16
I32 = jnp.int32

D_MODEL = 1024
HEAD_DIM = 64
N_HEADS_FOX = 6
N_HEADS_SB = 5
N_HEADS_DSA = 5
W_FOX = N_HEADS_FOX * HEAD_DIM
W_SB = N_HEADS_SB * HEAD_DIM
W_DSA = N_HEADS_DSA * HEAD_DIM
KV_LATENT = 128
N_IDX_HEADS = 4
IDX_DIM = 64
CHUNK = 64
TOPK_MAX = 256
ROPE_THETA = 500000.0
ROPE_DIM = HEAD_DIM // 4
D_FF = 2816
EPS = 1e-6

LANES = 128
PAIR_W = 2 * HEAD_DIM
W_PAIRS = 3 * PAIR_W
VMEM_LIMIT = 56 * 1024 * 1024

GA_W = 3 * W_FOX
GB_W = 3 * W_PAIRS
GC_ROPE_W = W_DSA + N_IDX_HEADS * IDX_DIM + IDX_DIM
GC_W = GC_ROPE_W + KV_LATENT
GS_W = LANES
GG_W = 3 * D_MODEL
GA_0 = 0
GB_0 = GA_0 + GA_W
GC_0 = GB_0 + GB_W
GS_0 = GC_0 + GC_W
GG_0 = GS_0 + GS_W
D_IN_PAD = GG_0 + GG_W

FF_CHUNK = 256
N_FF_CHUNKS = D_FF // FF_CHUNK

TM_FFN = 512
TM_PROJ = 256
TM_MERGE = 512
TQ = 256
QB_DSA = 128
KB_DSA = 256
NEG_BIG = -1e30
INT_MIN = -2147483648


def _cparams(sem):
    return pltpu.CompilerParams(dimension_semantics=sem, vmem_limit_bytes=VMEM_LIMIT)


def _resident(shape):
    nd = len(shape)
    return pl.BlockSpec(shape, lambda *_: (0,) * nd, pipeline_mode=pl.Buffered(1))


def _rms(x, g):
    ms = jnp.mean(x * x, axis=-1, keepdims=True)
    return x * lax.rsqrt(ms + EPS) * g


def _dot(a, b):
    return jnp.dot(a, b, preferred_element_type=F32)


def _dot_nt(a, b):
    return lax.dot_general(a, b, (((1,), (1,)), ((), ())), preferred_element_type=F32)


def _log_sigmoid(z):
    return jnp.minimum(z, 0.0) - jnp.log1p(jnp.exp(-jnp.abs(z)))


def _ffn_kernel(x_ref, g_ref, w1_ref, w2_ref, gf_ref, o_ref, acc_ref, *, final_norm):
    x = x_ref[...]
    h = _rms(x, g_ref[...]).astype(BF16)
    for j in range(N_FF_CHUNKS):
        gu = _dot(h, w1_ref[j])
        g = gu[:, :FF_CHUNK]
        u = gu[:, FF_CHUNK:]
        a = (g * jax.nn.sigmoid(g) * u).astype(BF16)
        d = _dot(a, w2_ref[j])
        if j == 0:
            acc_ref[...] = d
        else:
            acc_ref[...] += d
    y = x + 0.5 * acc_ref[...]
    if final_norm:
        y = _rms(y, gf_ref[...])
    o_ref[...] = y


def _ffn(x, g, w1, w2, gf, final_norm):
    m = x.shape[0]
    return pl.pallas_call(
        functools.partial(_ffn_kernel, final_norm=final_norm),
        grid=(m // TM_FFN,),
        in_specs=[
            pl.BlockSpec((TM_FFN, D_MODEL), lambda i: (i, 0)),
            _resident((1, D_MODEL)),
            _resident(w1.shape),
            _resident(w2.shape),
            _resident((1, D_MODEL)),
        ],
        out_specs=pl.BlockSpec((TM_FFN, D_MODEL), lambda i: (i, 0)),
        out_shape=jax.ShapeDtypeStruct((m, D_MODEL), F32),
        scratch_shapes=[pltpu.VMEM((TM_FFN, D_MODEL), F32)],
        compiler_params=_cparams(("arbitrary",)),
        name="ffn",
    )(x, g, w1, w2, gf)


def _rope_tile(t, c, s1, s2):
    return t * c + pltpu.roll(t, LANES - ROPE_DIM // 2, 1) * s1 + pltpu.roll(t, ROPE_DIM // 2, 1) * s2


def _proj_kernel(x_ref, g_ref, w_ref, wkv_ref, gkv_ref, gki_ref, rc_ref, rs1_ref, rs2_ref,
                 qa_ref, qb_ref, vat_ref, vbt_ref, dq_ref, dkk_ref, vt_ref, sm_ref, gt_ref):
    h = _rms(x_ref[...], g_ref[...]).astype(BF16)

    def proj(c0, width):
        return _dot(h, w_ref[:, c0:c0 + width])

    for g0, qk_ref, v_ref in ((GA_0, qa_ref, vat_ref), (GB_0, qb_ref, vbt_ref)):
        for c0 in range(0, 2 * W_PAIRS, W_PAIRS):
            qk_ref[:, c0:c0 + W_PAIRS] = proj(g0 + c0, W_PAIRS).astype(BF16)
        v = proj(g0 + 2 * W_PAIRS, W_PAIRS)
        for k in range(W_PAIRS // LANES):
            v_ref[k * LANES:(k + 1) * LANES, :] = v[:, k * LANES:(k + 1) * LANES].T.astype(BF16)
    sm_ref[...] = proj(GS_0, GS_W)
    for c0 in range(0, GG_W, 512):
        gt_ref[:, c0:c0 + 512] = proj(GG_0 + c0, 512)

    r = proj(GC_0, GC_W)
    lane = lax.broadcasted_iota(I32, (1, LANES), 1)
    hi_half = lane >= HEAD_DIM
    rc, rs1, rs2 = rc_ref[...], rs1_ref[...], rs2_ref[...]
    n_tiles = GC_ROPE_W // LANES
    last = None
    for k in range(n_tiles):
        t = r[:, k * LANES:(k + 1) * LANES]
        if k == n_tiles - 1:
            ms = jnp.sum(jnp.where(hi_half, t * t, 0.0), axis=-1, keepdims=True) * (1.0 / IDX_DIM)
            t = t * jnp.where(hi_half, lax.rsqrt(ms + EPS) * gki_ref[...], 1.0)
        t = _rope_tile(t, rc, rs1, rs2)
        dq_ref[:, k * LANES:(k + 1) * LANES] = t.astype(BF16)
        last = t
    ckv = _rms(r[:, GC_ROPE_W:GC_W], gkv_ref[...]).astype(BF16)
    kv = _dot(ckv, wkv_ref[...])
    lo_half = jnp.logical_not(hi_half)
    kv = _rope_tile(kv, jnp.where(lo_half, rc, 1.0), jnp.where(lo_half, rs1, 0.0),
                    jnp.where(lo_half, rs2, 0.0))
    dkk_ref[...] = jnp.where(lo_half, kv, last).astype(BF16)
    vt_ref[...] = kv.T[HEAD_DIM:, :].astype(BF16)


def _proj(x, g, w, wkv, gkv, gki, rc, rs1, rs2, batch, seq):
    m = x.shape[0]
    spb = seq // TM_PROJ
    row = lambda width: pl.BlockSpec((TM_PROJ, width), lambda i: (i, 0))
    pos = pl.BlockSpec((TM_PROJ, LANES), lambda i: (i % spb, 0))
    return pl.pallas_call(
        _proj_kernel,
        grid=(m // TM_PROJ,),
        in_specs=[
            row(D_MODEL), _resident((1, D_MODEL)), _resident(w.shape), _resident(wkv.shape),
            _resident((1, KV_LATENT)), _resident((1, LANES)), pos, pos, pos,
        ],
        out_specs=[
            row(2 * W_PAIRS), row(2 * W_PAIRS),
            pl.BlockSpec((None, W_PAIRS, TM_PROJ), lambda i: (i // spb, 0, i % spb)),
            pl.BlockSpec((None, W_PAIRS, TM_PROJ), lambda i: (i // spb, 0, i % spb)),
            row(GC_ROPE_W), row(LANES),
            pl.BlockSpec((None, HEAD_DIM, TM_PROJ), lambda i: (i // spb, 0, i % spb)),
            row(GS_W), row(GG_W),
        ],
        out_shape=[
            jax.ShapeDtypeStruct((m, 2 * W_PAIRS), BF16),
            jax.ShapeDtypeStruct((m, 2 * W_PAIRS), BF16),
            jax.ShapeDtypeStruct((batch, W_PAIRS, seq), BF16),
            jax.ShapeDtypeStruct((batch, W_PAIRS, seq), BF16),
            jax.ShapeDtypeStruct((m, GC_ROPE_W), BF16),
            jax.ShapeDtypeStruct((m, LANES), BF16),
            jax.ShapeDtypeStruct((batch, HEAD_DIM, seq), BF16),
            jax.ShapeDtypeStruct((m, GS_W), F32),
            jax.ShapeDtypeStruct((m, GG_W), F32),
        ],
        compiler_params=_cparams(("arbitrary",)),
        name="in_proj",
    )(x, g, w, wkv, gkv, gki, rc, rs1, rs2)


def _gate_kernel(s_ref, b_ref, c_ref, ct_ref, wt_ref):
    xt = s_ref[...].T
    c = _log_sigmoid(xt + b_ref[...])
    seq = c.shape[1]
    lane = lax.broadcasted_iota(I32, c.shape, 1)
    d = 1
    while d < seq:
        c = c + jnp.where(lane >= d, pltpu.roll(c, d, 1), 0.0)
        d *= 2
    ct_ref[...] = c[0:8]
    wt_ref[...] = xt[8:16]
    c_ref[...] = c.T


def _gate(small, bcol, batch, seq):
    return pl.pallas_call(
        _gate_kernel,
        grid=(batch,),
        in_specs=[pl.BlockSpec((seq, LANES), lambda b: (b, 0)), _resident((LANES, 1))],
        out_specs=[
            pl.BlockSpec((seq, LANES), lambda b: (b, 0)),
            pl.BlockSpec((None, 8, seq), lambda b: (b, 0, 0)),
            pl.BlockSpec((None, 8, seq), lambda b: (b, 0, 0)),
        ],
        out_shape=[
            jax.ShapeDtypeStruct((batch * seq, LANES), F32),
            jax.ShapeDtypeStruct((batch, 8, seq), F32),
            jax.ShapeDtypeStruct((batch, 8, seq), F32),
        ],
        compiler_params=_cparams(("arbitrary",)),
        name="gate_cumsum",
    )(small, bcol)


def _rows(blk):
    return pl.ds(pl.multiple_of(blk * TQ, TQ), TQ)


def _pipelined_blocks(i, heads, qk, vpu, pv, state0, acc0):
    n = len(heads)
    s_diag = [qk(h, i) for h in heads]
    s_next = tuple(qk(h, jnp.maximum(i - 1, 0)) for h in heads)
    first = [vpu(heads[k], i, s_diag[k], state0[k], True) for k in range(n)]
    state = tuple(f[0] for f in first)
    pend = tuple(f[1] for f in first)

    def body(t, carry):
        s_cur, state, pend, acc = carry
        blk = i - t
        s_nxt = tuple(qk(h, jnp.maximum(blk - 1, 0)) for h in heads)
        acc = tuple(pv(heads[k], blk + 1, pend[k], acc[k]) for k in range(n))
        new = [vpu(heads[k], blk, s_cur[k], state[k], False) for k in range(n)]
        return s_nxt, tuple(x[0] for x in new), tuple(x[1] for x in new), acc

    _, state, pend, acc = lax.fori_loop(1, i + 1, body, (s_next, state, pend, tuple(acc0)))
    acc = tuple(pv(heads[k], 0, pend[k], acc[k]) for k in range(n))
    return state, acc


def _pair_masks():
    lane = lax.broadcasted_iota(I32, (1, LANES), 1)
    sub = lax.broadcasted_iota(I32, (LANES, 1), 0)
    return (lane < HEAD_DIM, lane >= HEAD_DIM), sub < HEAD_DIM


def _fox_kernel(q_ref, k_ref, vt_ref, c_ref, ct_ref, o_ref):
    pair = pl.program_id(1)
    i = pl.program_id(2)
    q = q_ref[...]
    lane = lax.broadcasted_iota(I32, (1, LANES), 1)
    sub8 = lax.broadcasted_iota(I32, (8, 1), 0)
    halves, sub_lo = _pair_masks()
    causal_t = (lax.broadcasted_iota(I32, (TQ, TQ), 0) <= lax.broadcasted_iota(I32, (TQ, TQ), 1))
    qm = [jnp.where(h, q, jnp.zeros_like(q)) for h in halves]
    ct_q = ct_ref[:, _rows(i)]
    cq = [jnp.sum(jnp.where(sub8 == 2 * pair + e, ct_q, 0.0), axis=0, keepdims=True) for e in range(2)]

    def qk(e, blk):
        return _dot_nt(k_ref[_rows(blk), :], qm[e])

    def vpu(e, blk, s, state, masked):
        m, l = state
        ck = jnp.sum(jnp.where(lane == 2 * pair + e, c_ref[_rows(blk), :], 0.0), axis=1, keepdims=True)
        u = s - ck
        if masked:
            u = jnp.where(causal_t, u, -jnp.inf)
        m_new = jnp.maximum(m, jnp.max(u, axis=0, keepdims=True) + cq[e])
        alpha = jnp.exp(m - m_new)
        p = jnp.exp(u + (cq[e] - m_new))
        l = alpha * l + jnp.sum(p, axis=0, keepdims=True)
        return (m_new, l), (alpha, p.astype(BF16))

    def pv(e, blk, pend, acc):
        alpha, p = pend
        return alpha * acc + _dot(vt_ref[:, _rows(blk)], p)

    state0 = [(jnp.full((1, TQ), -jnp.inf, F32), jnp.zeros((1, TQ), F32))] * 2
    acc0 = [jnp.zeros((LANES, TQ), F32)] * 2
    state, acc = _pipelined_blocks(i, (0, 1), qk, vpu, pv, state0, acc0)
    outs = [acc[e] * (1.0 / state[e][1]) for e in range(2)]
    o_ref[...] = jnp.where(sub_lo, outs[0], outs[1]).T.astype(BF16)


def _fox(qkv, vt, c, ct, batch, seq):
    nq = seq // TQ
    n_pairs = N_HEADS_FOX // 2
    return pl.pallas_call(
        _fox_kernel,
        grid=(batch, n_pairs, nq),
        in_specs=[
            pl.BlockSpec((TQ, LANES), lambda b, p, i: (b * nq + i, p)),
            pl.BlockSpec((seq, LANES), lambda b, p, i: (b, n_pairs + p)),
            pl.BlockSpec((None, LANES, seq), lambda b, p, i: (b, p, 0)),
            pl.BlockSpec((seq, LANES), lambda b, p, i: (b, 0)),
            pl.BlockSpec((None, 8, seq), lambda b, p, i: (b, 0, 0)),
        ],
        out_specs=pl.BlockSpec((TQ, LANES), lambda b, p, i: (b * nq + i, p)),
        out_shape=jax.ShapeDtypeStruct((batch * seq, W_PAIRS), BF16),
        compiler_params=_cparams(("arbitrary", "arbitrary", "arbitrary")),
        name="fox_attn",
    )(qkv, qkv, vt, c, ct)


def _sb_kernel(q_ref, k_ref, vt_ref, o_ref):
    pair = pl.program_id(1)
    i = pl.program_id(2)
    q = q_ref[...]
    halves, sub_lo = _pair_masks()
    row = lax.broadcasted_iota(I32, (TQ, TQ), 0)
    col = lax.broadcasted_iota(I32, (TQ, TQ), 1)
    strict_t = row < col
    from_here = (col >= row).astype(BF16)
    from_here2 = jnp.concatenate([from_here, from_here], axis=1)
    qm = [jnp.where(h, q, jnp.zeros_like(q)) for h in halves]

    def qk(e, blk):
        return _dot_nt(k_ref[_rows(blk), :], qm[e])

    def vpu(e, blk, z, tail, masked):
        nz = -z
        l1m = jnp.minimum(nz, 0.0) - jnp.log(1.0 + jnp.exp(jnp.minimum(z, nz)))
        if masked:
            l1m = jnp.where(strict_t, l1m, 0.0)
        hi = l1m.astype(BF16)
        lo = (l1m - hi.astype(F32)).astype(BF16)
        run_sum = _dot(from_here2, jnp.concatenate([hi, lo], axis=0))
        a = jnp.exp(z + run_sum + tail)
        if masked:
            a = jnp.where(strict_t, a, 0.0)
        return tail + jnp.sum(l1m, axis=0, keepdims=True), a.astype(BF16)

    def pv(e, blk, a, acc):
        return acc + _dot(vt_ref[:, _rows(blk)], a)

    def run(heads):
        n = len(heads)
        _, acc = _pipelined_blocks(i, heads, qk, vpu, pv, [jnp.zeros((1, TQ), F32)] * n,
                                   [jnp.zeros((LANES, TQ), F32)] * n)
        return acc

    @pl.when(pair < N_HEADS_SB // 2)
    def _():
        acc = run((0, 1))
        o_ref[...] = jnp.where(sub_lo, acc[0], acc[1]).T.astype(BF16)

    @pl.when(pair >= N_HEADS_SB // 2)
    def _():
        acc = run((0,))
        o_ref[...] = jnp.where(sub_lo, acc[0], 0.0).T.astype(BF16)


def _sb(qkv, vt, batch, seq):
    nq = seq // TQ
    n_pairs = W_PAIRS // PAIR_W
    return pl.pallas_call(
        _sb_kernel,
        grid=(batch, n_pairs, nq),
        in_specs=[
            pl.BlockSpec((TQ, LANES), lambda b, p, i: (b * nq + i, p)),
            pl.BlockSpec((seq, LANES), lambda b, p, i: (b, n_pairs + p)),
            pl.BlockSpec((None, LANES, seq), lambda b, p, i: (b, p, 0)),
        ],
        out_specs=pl.BlockSpec((TQ, LANES), lambda b, p, i: (b * nq + i, p)),
        out_shape=jax.ShapeDtypeStruct((batch * seq, W_PAIRS), BF16),
        compiler_params=_cparams(("arbitrary", "arbitrary", "arbitrary")),
        name="sb_attn",
    )(qkv, qkv, vt)


def _dsa_kernel(q_ref, kk_ref, vt_ref, wt_ref, y_ref, key_ref):
    i = pl.program_id(1)
    n_kb = i // 2 + 1
    k_sel = jnp.minimum(TOPK_MAX, QB_DSA * (i + 1))
    lane = lax.broadcasted_iota(I32, (1, LANES), 1)
    lo_half = lane < HEAD_DIM

    qb = q_ref[...].astype(F32)
    tiles = [qb[:, k * LANES:(k + 1) * LANES] for k in range(GC_ROPE_W // LANES)]
    swap = lambda t: pltpu.roll(t, HEAD_DIM, 1)
    keep_lo = lambda t: jnp.where(lo_half, t, 0.0)
    keep_hi = lambda t: jnp.where(lo_half, 0.0, t)
    r_att = jnp.concatenate(
        [keep_lo(tiles[0]), keep_lo(swap(tiles[0])), keep_lo(tiles[1]), keep_lo(swap(tiles[1])),
         keep_lo(tiles[2])], axis=0).astype(BF16)
    r_idx = jnp.concatenate(
        [keep_hi(tiles[2]), keep_hi(swap(tiles[3])), keep_hi(tiles[3]), keep_hi(swap(tiles[4]))],
        axis=0).astype(BF16)
    wt = wt_ref[...]
    q_chunk = (i * QB_DSA + lane) // CHUNK
    k_chunk0 = lax.broadcasted_iota(I32, (KB_DSA, 1), 0) // CHUNK

    def kslice(jb):
        return pl.ds(pl.multiple_of(jb * KB_DSA, KB_DSA), KB_DSA)

    def score_block(jb, _):
        raw = _dot_nt(kk_ref[kslice(jb), :], r_idx)
        sc = jnp.zeros((KB_DSA, LANES), F32)
        for h in range(N_IDX_HEADS):
            sc = sc + jnp.maximum(raw[:, h * LANES:(h + 1) * LANES], 0.0) * wt[h:h + 1, :]
        adm = (k_chunk0 + jb * (KB_DSA // CHUNK)) <= q_chunk
        sc = jnp.where(adm, sc, -jnp.inf)
        bits = lax.bitcast_convert_type(sc, I32)
        key = jnp.where(bits < 0, bits ^ 0x7FFFFFFF, bits)
        key_ref[kslice(jb), :] = jnp.where(sc == 0.0, 0, key)
        return 0

    lax.fori_loop(0, n_kb, score_block, 0)

    def select_threshold(n):
        def count_ge(cand):
            acc = jnp.zeros((8, LANES), I32)
            for jb in range(n):
                ge = (key_ref[jb * KB_DSA:(jb + 1) * KB_DSA, :] >= cand).astype(I32)
                acc = acc + jnp.sum(ge.reshape(KB_DSA // 8, 8, LANES), axis=0)
            return jnp.sum(acc, axis=0, keepdims=True)

        def bisect(it, lo):
            cand = lo + jnp.left_shift(jnp.int32(1), 31 - it)
            return jnp.where(count_ge(cand) >= k_sel, cand, lo)

        thr = lax.fori_loop(0, 32, bisect, jnp.full((1, LANES), INT_MIN, I32))
        return thr, (k_sel - count_ge(thr + 1)).astype(F32)

    max_kb = key_ref.shape[0] // KB_DSA
    thr, need = lax.switch(n_kb - 1, [functools.partial(select_threshold, n) for n in range(1, max_kb + 1)])
    neg_inf_key = jnp.int32(-8388608) ^ 0x7FFFFFFF
    before = (lax.broadcasted_iota(I32, (KB_DSA, KB_DSA), 1)
              < lax.broadcasted_iota(I32, (KB_DSA, KB_DSA), 0)).astype(BF16)

    def scores(jb, ties):
        keyb = key_ref[kslice(jb), :]
        eq = keyb == thr
        eqf = jnp.where(eq, 1.0, 0.0)
        rank = _dot(before, eqf.astype(BF16)) + ties
        sel = ((keyb > thr) | (eq & (rank < need))) & (keyb > neg_inf_key)
        bias = jnp.where(sel, 0.0, NEG_BIG)
        st = _dot_nt(kk_ref[kslice(jb), :], r_att)
        st = st + jnp.concatenate([bias] * N_HEADS_DSA, axis=1)
        return st, ties + jnp.sum(eqf, axis=0, keepdims=True)

    def softmax_step(st, m, l):
        m_new = jnp.maximum(m, jnp.max(st, axis=0, keepdims=True))
        alpha = jnp.exp(m - m_new)
        p = jnp.exp(st - m_new)
        return m_new, alpha * l + jnp.sum(p, axis=0, keepdims=True), (alpha, p.astype(BF16))

    def values(jb, pend, acc):
        alpha, p = pend
        return alpha * acc + _dot(vt_ref[:, kslice(jb)], p)

    wide = N_HEADS_DSA * LANES

    def attend(jb, carry):
        ties, m, l, acc = carry
        st, ties = scores(jb, ties)
        m, l, pend = softmax_step(st, m, l)
        return ties, m, l, values(jb, pend, acc)

    _, _, l, acc = lax.fori_loop(
        0, n_kb, attend, (jnp.zeros((1, LANES), F32), jnp.full((1, wide), NEG_BIG, F32),
                          jnp.zeros((1, wide), F32), jnp.zeros((HEAD_DIM, wide), F32)))
    o = acc * (1.0 / l)
    heads = [o[:, h * LANES:(h + 1) * LANES] for h in range(N_HEADS_DSA)]
    heads.append(jnp.zeros_like(heads[0]))
    for g in range(W_PAIRS // PAIR_W):
        y_ref[:, g * LANES:(g + 1) * LANES] = jnp.concatenate(
            [heads[2 * g], heads[2 * g + 1]], axis=0).T.astype(BF16)


def _dsa(dq, dkk, vt, wt, batch, seq):
    nq = seq // QB_DSA
    return pl.pallas_call(
        _dsa_kernel,
        grid=(batch, nq),
        in_specs=[
            pl.BlockSpec((QB_DSA, GC_ROPE_W), lambda b, i: (b * nq + i, 0)),
            pl.BlockSpec((seq, LANES), lambda b, i: (b, 0)),
            pl.BlockSpec((None, HEAD_DIM, seq), lambda b, i: (b, 0, 0)),
            pl.BlockSpec((None, 8, QB_DSA), lambda b, i: (b, 0, i)),
        ],
        out_specs=pl.BlockSpec((QB_DSA, W_PAIRS), lambda b, i: (b * nq + i, 0)),
        out_shape=jax.ShapeDtypeStruct((batch * seq, W_PAIRS), BF16),
        scratch_shapes=[pltpu.VMEM((seq, LANES), I32)],
        compiler_params=_cparams(("arbitrary", "arbitrary")),
        name="dsa_attn",
    )(dq, dkk, vt, wt)


def _merge_kernel(x_ref, ya_ref, yb_ref, yc_ref, gt_ref, wa_ref, wb_ref, wc_ref, wo_ref, o_ref):
    merged = None
    for k, (y_ref, w_ref) in enumerate(((ya_ref, wa_ref), (yb_ref, wb_ref), (yc_ref, wc_ref))):
        gate = jax.nn.sigmoid(gt_ref[:, k * D_MODEL:(k + 1) * D_MODEL])
        term = gate * _dot(y_ref[...], w_ref[...])
        merged = term if merged is None else merged + term
    o_ref[...] = x_ref[...] + _dot(merged.astype(BF16), wo_ref[...])


def _merge(x, ya, yb, yc, gates, wa, wb, wc, wo):
    m = x.shape[0]
    row = lambda width: pl.BlockSpec((TM_MERGE, width), lambda i: (i, 0))
    return pl.pallas_call(
        _merge_kernel,
        grid=(m // TM_MERGE,),
        in_specs=[row(D_MODEL), row(W_PAIRS), row(W_PAIRS), row(W_PAIRS), row(GG_W),
                  _resident(wa.shape), _resident(wb.shape), _resident(wc.shape), _resident(wo.shape)],
        out_specs=row(D_MODEL),
        out_shape=jax.ShapeDtypeStruct((m, D_MODEL), F32),
        compiler_params=_cparams(("arbitrary",)),
        name="merge_out",
    )(x, ya, yb, yc, gates, wa, wb, wc, wo)


def _pad_heads(w, axis):
    pad = [(0, 0)] * w.ndim
    pad[axis] = (0, W_PAIRS - w.shape[axis])
    return jnp.pad(w, pad)


def _layout_w_in(w_in):
    sizes = [W_FOX, W_FOX, W_FOX, N_HEADS_FOX, W_SB, W_SB, W_SB, W_DSA, KV_LATENT,
             N_IDX_HEADS * IDX_DIM, IDX_DIM, N_IDX_HEADS, 3 * D_MODEL]
    pieces, acc = [], 0
    for s in sizes:
        pieces.append(w_in[:, acc:acc + s])
        acc += s
    q_a, k_a, v_a, f_a, q_b, k_b, v_b, q_c, c_kv, q_i, k_i, w_i, gates = pieces
    scale = HEAD_DIM ** -0.5
    zeros = lambda n: jnp.zeros((D_MODEL, n), w_in.dtype)
    cols = [q_a * scale, k_a, v_a,
            _pad_heads(q_b * scale, 1), _pad_heads(k_b, 1), _pad_heads(v_b, 1),
            q_c * scale, q_i, k_i, c_kv,
            f_a, zeros(8 - N_HEADS_FOX), w_i, zeros(GS_W - 8 - N_IDX_HEADS),
            gates]
    out = jnp.concatenate(cols, axis=1).astype(BF16)
    assert out.shape == (D_MODEL, D_IN_PAD)
    return out


def _layout_ffn(w_gu, w_down):
    wg = w_gu[:, :D_FF].reshape(D_MODEL, N_FF_CHUNKS, FF_CHUNK)
    wu = w_gu[:, D_FF:].reshape(D_MODEL, N_FF_CHUNKS, FF_CHUNK)
    w1 = jnp.concatenate([wg, wu], axis=2).transpose(1, 0, 2).astype(BF16)
    w2 = w_down.reshape(N_FF_CHUNKS, FF_CHUNK, D_MODEL).astype(BF16)
    return w1, w2


def _rope_tables(seq):
    half = ROPE_DIM // 2
    pos = jnp.arange(seq, dtype=F32)
    inv = ROPE_THETA ** (-jnp.arange(0, ROPE_DIM, 2, dtype=F32) / ROPE_DIM)
    ang = pos[:, None] * inv[None, :]
    cos, sin = jnp.cos(ang), jnp.sin(ang)
    ones = jnp.ones((seq, HEAD_DIM - ROPE_DIM), F32)
    zeros_rest = jnp.zeros((seq, HEAD_DIM - ROPE_DIM), F32)
    zeros_half = jnp.zeros((seq, half), F32)
    c = jnp.concatenate([cos, cos, ones], axis=1)
    s1 = jnp.concatenate([-sin, zeros_half, zeros_rest], axis=1)
    s2 = jnp.concatenate([zeros_half, sin, zeros_rest], axis=1)
    two = lambda t: jnp.concatenate([t, t], axis=1)
    return two(c), two(s1), two(s2)


@jax.jit
def kernel(x, g_ffn1, w_ffn1_gu, w_ffn1_down, g_mix, w_in, b_forget, g_kv_latent, w_kv_up, g_idx_k,
           w_up_fox, w_up_sb, w_up_dsa, w_out, g_ffn2, w_ffn2_gu, w_ffn2_down, g_final):
    batch, seq, _ = x.shape
    depth = g_ffn1.shape[0]
    rc, rs1, rs2 = _rope_tables(seq)
    xf = x.reshape(batch * seq, D_MODEL)
    row = lambda v: v.reshape(1, -1).astype(F32)
    gf = row(g_final)
    for layer in range(depth):
        w1, w2 = _layout_ffn(w_ffn1_gu[layer], w_ffn1_down[layer])
        xf = _ffn(xf, row(g_ffn1[layer]), w1, w2, gf, False)

        gki = jnp.concatenate([jnp.ones((IDX_DIM,), F32), g_idx_k[layer]]).reshape(1, LANES)
        qk_a, qk_b, vt_a, vt_b, dq, dkk, vt, small, gates = _proj(
            xf, row(g_mix[layer]), _layout_w_in(w_in[layer]), w_kv_up[layer].astype(BF16),
            row(g_kv_latent[layer]), gki, rc, rs1, rs2, batch, seq)
        bcol = jnp.zeros((LANES, 1), F32).at[:N_HEADS_FOX, 0].set(b_forget[layer])
        c, ct, wt = _gate(small, bcol, batch, seq)
        y_a = _fox(qk_a, vt_a, c, ct, batch, seq)
        y_b = _sb(qk_b, vt_b, batch, seq)
        y_c = _dsa(dq, dkk, vt, wt, batch, seq)
        xf = _merge(xf, y_a, y_b, y_c, gates,
                    w_up_fox[layer].astype(BF16), _pad_heads(w_up_sb[layer], 0).astype(BF16),
                    _pad_heads(w_up_dsa[layer], 0).astype(BF16), w_out[layer].astype(BF16))

        w1, w2 = _layout_ffn(w_ffn2_gu[layer], w_ffn2_down[layer])
        xf = _ffn(xf, row(g_ffn2[layer]), w1, w2, gf, layer == depth - 1)
    return xf.reshape(batch, seq, D_MODEL)
```

```python
import functools

import jax
import jax.numpy as jnp
from jax import lax
from jax.experimental import pallas as pl
from jax.experimental.pallas import tpu as pltpu

F32 = jnp.float32
BF16 = jnp.bfloat16
I32 = jnp.int32

D_MODEL = 1024
HEAD_DIM = 64
N_HEADS_FOX = 6
N_HEADS_SB = 5
N_HEADS_DSA = 5
W_FOX = N_HEADS_FOX * HEAD_DIM
W_SB = N_HEADS_SB * HEAD_DIM
W_DSA = N_HEADS_DSA * HEAD_DIM
KV_LATENT = 128
N_IDX_HEADS = 4
IDX_DIM = 64
CHUNK = 64
TOPK_MAX = 256
ROPE_THETA = 500000.0
ROPE_DIM = HEAD_DIM // 4
D_FF = 2816
EPS = 1e-6

LANES = 128
PAIR_W = 2 * HEAD_DIM
W_PAIRS = 3 * PAIR_W
VMEM_LIMIT = 56 * 1024 * 1024

GA_W = 3 * W_FOX
GB_W = 3 * W_PAIRS
GC_ROPE_W = W_DSA + N_IDX_HEADS * IDX_DIM + IDX_DIM
GC_W = GC_ROPE_W + KV_LATENT
GS_W = LANES
GG_W = 3 * D_MODEL
GA_0 = 0
GB_0 = GA_0 + GA_W
GC_0 = GB_0 + GB_W
GS_0 = GC_0 + GC_W
GG_0 = GS_0 + GS_W
D_IN_PAD = GG_0 + GG_W

FF_CHUNK = 256
N_FF_CHUNKS = D_FF // FF_CHUNK

TM_FFN = 512
TM_PROJ = 256
TM_MERGE = 512
TQ = 256
QB_DSA = 128
KB_DSA = 256
NEG_BIG = -1e30
INT_MIN = -2147483648


def _cparams(sem):
    return pltpu.CompilerParams(dimension_semantics=sem, vmem_limit_bytes=VMEM_LIMIT)


def _resident(shape):
    nd = len(shape)
    return pl.BlockSpec(shape, lambda *_: (0,) * nd, pipeline_mode=pl.Buffered(1))


def _rms(x, g):
    ms = jnp.mean(x * x, axis=-1, keepdims=True)
    return x * lax.rsqrt(ms + EPS) * g


def _dot(a, b):
    return jnp.dot(a, b, preferred_element_type=F32)


def _dot_nt(a, b):
    return lax.dot_general(a, b, (((1,), (1,)), ((), ())), preferred_element_type=F32)


def _log_sigmoid(z):
    return jnp.minimum(z, 0.0) - jnp.log1p(jnp.exp(-jnp.abs(z)))


def _ffn_kernel(x_ref, g_ref, w1_ref, w2_ref, gf_ref, o_ref, acc_ref, *, final_norm):
    x = x_ref[...]
    h = _rms(x, g_ref[...]).astype(BF16)
    for j in range(N_FF_CHUNKS):
        gu = _dot(h, w1_ref[j])
        g = gu[:, :FF_CHUNK]
        u = gu[:, FF_CHUNK:]
        a = (g * jax.nn.sigmoid(g) * u).astype(BF16)
        d = _dot(a, w2_ref[j])
        if j == 0:
            acc_ref[...] = d
        else:
            acc_ref[...] += d
    y = x + 0.5 * acc_ref[...]
    if final_norm:
        y = _rms(y, gf_ref[...])
    o_ref[...] = y


def _ffn(x, g, w1, w2, gf, final_norm):
    m = x.shape[0]
    return pl.pallas_call(
        functools.partial(_ffn_kernel, final_norm=final_norm),
        grid=(m // TM_FFN,),
        in_specs=[
            pl.BlockSpec((TM_FFN, D_MODEL), lambda i: (i, 0)),
            _resident((1, D_MODEL)),
            _resident(w1.shape),
            _resident(w2.shape),
            _resident((1, D_MODEL)),
        ],
        out_specs=pl.BlockSpec((TM_FFN, D_MODEL), lambda i: (i, 0)),
        out_shape=jax.ShapeDtypeStruct((m, D_MODEL), F32),
        scratch_shapes=[pltpu.VMEM((TM_FFN, D_MODEL), F32)],
        compiler_params=_cparams(("arbitrary",)),
        name="ffn",
    )(x, g, w1, w2, gf)


def _rope_tile(t, c, s1, s2):
    return t * c + pltpu.roll(t, LANES - ROPE_DIM // 2, 1) * s1 + pltpu.roll(t, ROPE_DIM // 2, 1) * s2


def _proj_kernel(x_ref, g_ref, w_ref, wkv_ref, gkv_ref, gki_ref, rc_ref, rs1_ref, rs2_ref,
                 qa_ref, qb_ref, vat_ref, vbt_ref, dq_ref, dkk_ref, vt_ref, sm_ref, gt_ref):
    h = _rms(x_ref[...], g_ref[...]).astype(BF16)

    def proj(c0, width):
        return _dot(h, w_ref[:, c0:c0 + width])

    for g0, qk_ref, v_ref in ((GA_0, qa_ref, vat_ref), (GB_0, qb_ref, vbt_ref)):
        for c0 in range(0, 2 * W_PAIRS, W_PAIRS):
            qk_ref[:, c0:c0 + W_PAIRS] = proj(g0 + c0, W_PAIRS).astype(BF16)
        v = proj(g0 + 2 * W_PAIRS, W_PAIRS)
        for k in range(W_PAIRS // LANES):
            v_ref[k * LANES:(k + 1) * LANES, :] = v[:, k * LANES:(k + 1) * LANES].T.astype(BF16)
    sm_ref[...] = proj(GS_0, GS_W)
    for c0 in range(0, GG_W, 512):
        gt_ref[:, c0:c0 + 512] = jax.nn.sigmoid(proj(GG_0 + c0, 512)).astype(BF16)

    r = proj(GC_0, GC_W)
    lane = lax.broadcasted_iota(I32, (1, LANES), 1)
    hi_half = lane >= HEAD_DIM
    rc, rs1, rs2 = rc_ref[...], rs1_ref[...], rs2_ref[...]
    n_tiles = GC_ROPE_W // LANES
    last = None
    for k in range(n_tiles):
        t = r[:, k * LANES:(k + 1) * LANES]
        if k == n_tiles - 1:
            ms = jnp.sum(jnp.where(hi_half, t * t, 0.0), axis=-1, keepdims=True) * (1.0 / IDX_DIM)
            t = t * jnp.where(hi_half, lax.rsqrt(ms + EPS) * gki_ref[...], 1.0)
        t = _rope_tile(t, rc, rs1, rs2)
        dq_ref[:, k * LANES:(k + 1) * LANES] = t.astype(BF16)
        last = t
    ckv = _rms(r[:, GC_ROPE_W:GC_W], gkv_ref[...]).astype(BF16)
    kv = _dot(ckv, wkv_ref[...])
    lo_half = jnp.logical_not(hi_half)
    kv = _rope_tile(kv, jnp.where(lo_half, rc, 1.0), jnp.where(lo_half, rs1, 0.0),
                    jnp.where(lo_half, rs2, 0.0))
    dkk_ref[...] = jnp.where(lo_half, kv, last).astype(BF16)
    vt_ref[...] = kv.T[HEAD_DIM:, :].astype(BF16)


def _proj(x, g, w, wkv, gkv, gki, rc, rs1, rs2, batch, seq):
    m = x.shape[0]
    spb = seq // TM_PROJ
    row = lambda width: pl.BlockSpec((TM_PROJ, width), lambda i: (i, 0))
    pos = pl.BlockSpec((TM_PROJ, LANES), lambda i: (i % spb, 0))
    return pl.pallas_call(
        _proj_kernel,
        grid=(m // TM_PROJ,),
        in_specs=[
            row(D_MODEL), _resident((1, D_MODEL)), _resident(w.shape), _resident(wkv.shape),
            _resident((1, KV_LATENT)), _resident((1, LANES)), pos, pos, pos,
        ],
        out_specs=[
            row(2 * W_PAIRS), row(2 * W_PAIRS),
            pl.BlockSpec((None, W_PAIRS, TM_PROJ), lambda i: (i // spb, 0, i % spb)),
            pl.BlockSpec((None, W_PAIRS, TM_PROJ), lambda i: (i // spb, 0, i % spb)),
            row(GC_ROPE_W), row(LANES),
            pl.BlockSpec((None, HEAD_DIM, TM_PROJ), lambda i: (i // spb, 0, i % spb)),
            row(GS_W), row(GG_W),
        ],
        out_shape=[
            jax.ShapeDtypeStruct((m, 2 * W_PAIRS), BF16),
            jax.ShapeDtypeStruct((m, 2 * W_PAIRS), BF16),
            jax.ShapeDtypeStruct((batch, W_PAIRS, seq), BF16),
            jax.ShapeDtypeStruct((batch, W_PAIRS, seq), BF16),
            jax.ShapeDtypeStruct((m, GC_ROPE_W), BF16),
            jax.ShapeDtypeStruct((m, LANES), BF16),
            jax.ShapeDtypeStruct((batch, HEAD_DIM, seq), BF16),
            jax.ShapeDtypeStruct((m, GS_W), F32),
            jax.ShapeDtypeStruct((m, GG_W), BF16),
        ],
        compiler_params=_cparams(("arbitrary",)),
        name="in_proj",
    )(x, g, w, wkv, gkv, gki, rc, rs1, rs2)


def _gate_kernel(s_ref, b_ref, c_ref, ct_ref, wt_ref):
    xt = s_ref[...].T
    c = _log_sigmoid(xt + b_ref[...])
    seq = c.shape[1]
    lane = lax.broadcasted_iota(I32, c.shape, 1)
    d = 1
    while d < seq:
        c = c + jnp.where(lane >= d, pltpu.roll(c, d, 1), 0.0)
        d *= 2
    ct_ref[...] = c[0:8]
    wt_ref[...] = xt[8:16]
    c_ref[...] = c.T


def _gate(small, bcol, batch, seq):
    return pl.pallas_call(
        _gate_kernel,
        grid=(batch,),
        in_specs=[pl.BlockSpec((seq, LANES), lambda b: (b, 0)), _resident((LANES, 1))],
        out_specs=[
            pl.BlockSpec((seq, LANES), lambda b: (b, 0)),
            pl.BlockSpec((None, 8, seq), lambda b: (b, 0, 0)),
            pl.BlockSpec((None, 8, seq), lambda b: (b, 0, 0)),
        ],
        out_shape=[
            jax.ShapeDtypeStruct((batch * seq, LANES), F32),
            jax.ShapeDtypeStruct((batch, 8, seq), F32),
            jax.ShapeDtypeStruct((batch, 8, seq), F32),
        ],
        compiler_params=_cparams(("arbitrary",)),
        name="gate_cumsum",
    )(small, bcol)


def _rows(blk):
    return pl.ds(pl.multiple_of(blk * TQ, TQ), TQ)


def _pipelined_blocks(i, heads, qk, vpu, pv, state0, acc0):
    n = len(heads)
    s_diag = [qk(h, i) for h in heads]
    s_next = tuple(qk(h, jnp.maximum(i - 1, 0)) for h in heads)
    first = [vpu(heads[k], i, s_diag[k], state0[k], True) for k in range(n)]
    state = tuple(f[0] for f in first)
    pend = tuple(f[1] for f in first)

    def body(t, carry):
        s_cur, state, pend, acc = carry
        blk = i - t
        s_nxt = tuple(qk(h, jnp.maximum(blk - 1, 0)) for h in heads)
        acc = tuple(pv(heads[k], blk + 1, pend[k], acc[k]) for k in range(n))
        new = [vpu(heads[k], blk, s_cur[k], state[k], False) for k in range(n)]
        return s_nxt, tuple(x[0] for x in new), tuple(x[1] for x in new), acc

    _, state, pend, acc = lax.fori_loop(1, i + 1, body, (s_next, state, pend, tuple(acc0)))
    acc = tuple(pv(heads[k], 0, pend[k], acc[k]) for k in range(n))
    return state, acc


def _pair_masks():
    lane = lax.broadcasted_iota(I32, (1, LANES), 1)
    sub = lax.broadcasted_iota(I32, (LANES, 1), 0)
    return (lane < HEAD_DIM, lane >= HEAD_DIM), sub < HEAD_DIM


def _fox_kernel(q_ref, k_ref, vt_ref, c_ref, ct_ref, o_ref):
    pair = pl.program_id(1)
    i = pl.program_id(2)
    q = q_ref[...]
    lane = lax.broadcasted_iota(I32, (1, LANES), 1)
    sub8 = lax.broadcasted_iota(I32, (8, 1), 0)
    halves, sub_lo = _pair_masks()
    causal_t = (lax.broadcasted_iota(I32, (TQ, TQ), 0) <= lax.broadcasted_iota(I32, (TQ, TQ), 1))
    qm = [jnp.where(h, q, jnp.zeros_like(q)) for h in halves]
    ct_q = ct_ref[:, _rows(i)]
    cq = [jnp.sum(jnp.where(sub8 == 2 * pair + e, ct_q, 0.0), axis=0, keepdims=True) for e in range(2)]

    def qk(e, blk):
        return _dot_nt(k_ref[_rows(blk), :], qm[e])

    def vpu(e, blk, s, state, masked):
        m, l = state
        ck = jnp.sum(jnp.where(lane == 2 * pair + e, c_ref[_rows(blk), :], 0.0), axis=1, keepdims=True)
        u = s - ck
        if masked:
            u = jnp.where(causal_t, u, -jnp.inf)
        m_new = jnp.maximum(m, jnp.max(u, axis=0, keepdims=True) + cq[e])
        alpha = jnp.exp(m - m_new)
        p = jnp.exp(u + (cq[e] - m_new))
        l = alpha * l + jnp.sum(p, axis=0, keepdims=True)
        return (m_new, l), (alpha, p.astype(BF16))

    def pv(e, blk, pend, acc):
        alpha, p = pend
        return alpha * acc + _dot(vt_ref[:, _rows(blk)], p)

    state0 = [(jnp.full((1, TQ), -jnp.inf, F32), jnp.zeros((1, TQ), F32))] * 2
    acc0 = [jnp.zeros((LANES, TQ), F32)] * 2
    state, acc = _pipelined_blocks(i, (0, 1), qk, vpu, pv, state0, acc0)
    outs = [acc[e] * (1.0 / state[e][1]) for e in range(2)]
    o_ref[...] = jnp.where(sub_lo, outs[0], outs[1]).T.astype(BF16)


def _fox(qkv, vt, c, ct, batch, seq):
    nq = seq // TQ
    n_pairs = N_HEADS_FOX // 2
    return pl.pallas_call(
        _fox_kernel,
        grid=(batch, n_pairs, nq),
        in_specs=[
            pl.BlockSpec((TQ, LANES), lambda b, p, i: (b * nq + i, p)),
            pl.BlockSpec((seq, LANES), lambda b, p, i: (b, n_pairs + p)),
            pl.BlockSpec((None, LANES, seq), lambda b, p, i: (b, p, 0)),
            pl.BlockSpec((seq, LANES), lambda b, p, i: (b, 0)),
            pl.BlockSpec((None, 8, seq), lambda b, p, i: (b, 0, 0)),
        ],
        out_specs=pl.BlockSpec((TQ, LANES), lambda b, p, i: (b * nq + i, p)),
        out_shape=jax.ShapeDtypeStruct((batch * seq, W_PAIRS), BF16),
        compiler_params=_cparams(("arbitrary", "arbitrary", "arbitrary")),
        name="fox_attn",
    )(qkv, qkv, vt, c, ct)


def _sb_kernel(q_ref, k_ref, vt_ref, o_ref):
    pair = pl.program_id(1)
    i = pl.program_id(2)
    q = q_ref[...]
    halves, sub_lo = _pair_masks()
    row = lax.broadcasted_iota(I32, (TQ, TQ), 0)
    col = lax.broadcasted_iota(I32, (TQ, TQ), 1)
    strict_t = row < col
    half_blk = TQ // 2
    tri = (col >= row)[:half_blk, :half_blk].astype(BF16)
    tri2 = jnp.concatenate([tri, tri], axis=1)
    qm = [jnp.where(h, q, jnp.zeros_like(q)) for h in halves]

    def qk(e, blk):
        return _dot_nt(k_ref[_rows(blk), :], qm[e])

    def vpu(e, blk, z, tail, masked):
        nz = -z
        l1m = jnp.minimum(nz, 0.0) - jnp.log(1.0 + jnp.exp(jnp.minimum(z, nz)))
        if masked:
            l1m = jnp.where(strict_t, l1m, 0.0)
        hi = l1m.astype(BF16)
        lo = (l1m - hi.astype(F32)).astype(BF16)
        sum_a = jnp.sum(l1m[:half_blk], axis=0, keepdims=True)
        sum_b = jnp.sum(l1m[half_blk:], axis=0, keepdims=True)
        run_a = _dot(tri2, jnp.concatenate([hi[:half_blk], lo[:half_blk]], axis=0)) + sum_b
        run_b = _dot(tri2, jnp.concatenate([hi[half_blk:], lo[half_blk:]], axis=0))
        a = jnp.exp(z + jnp.concatenate([run_a, run_b], axis=0) + tail)
        if masked:
            a = jnp.where(strict_t, a, 0.0)
        return tail + (sum_a + sum_b), a.astype(BF16)

    def pv(e, blk, a, acc):
        return acc + _dot(vt_ref[:, _rows(blk)], a)

    def run(heads):
        n = len(heads)
        _, acc = _pipelined_blocks(i, heads, qk, vpu, pv, [jnp.zeros((1, TQ), F32)] * n,
                                   [jnp.zeros((LANES, TQ), F32)] * n)
        return acc

    @pl.when(pair < N_HEADS_SB // 2)
    def _():
        acc = run((0, 1))
        o_ref[...] = jnp.where(sub_lo, acc[0], acc[1]).T.astype(BF16)

    @pl.when(pair >= N_HEADS_SB // 2)
    def _():
        acc = run((0,))
        o_ref[...] = jnp.where(sub_lo, acc[0], 0.0).T.astype(BF16)


def _sb(qkv, vt, batch, seq):
    nq = seq // TQ
    n_pairs = W_PAIRS // PAIR_W
    return pl.pallas_call(
        _sb_kernel,
        grid=(batch, n_pairs, nq),
        in_specs=[
            pl.BlockSpec((TQ, LANES), lambda b, p, i: (b * nq + i, p)),
            pl.BlockSpec((seq, LANES), lambda b, p, i: (b, n_pairs + p)),
            pl.BlockSpec((None, LANES, seq), lambda b, p, i: (b, p, 0)),
        ],
        out_specs=pl.BlockSpec((TQ, LANES), lambda b, p, i: (b * nq + i, p)),
        out_shape=jax.ShapeDtypeStruct((batch * seq, W_PAIRS), BF16),
        compiler_params=_cparams(("arbitrary", "arbitrary", "arbitrary")),
        name="sb_attn",
    )(qkv, qkv, vt)


def _dsa_kernel(q_ref, kk_ref, vt_ref, wt_ref, y_ref, key_ref, *, n_kb):
    i = 2 * (n_kb - 1) + pl.program_id(1)
    k_sel = jnp.minimum(TOPK_MAX, QB_DSA * (i + 1))
    lane = lax.broadcasted_iota(I32, (1, LANES), 1)
    lo_half = lane < HEAD_DIM

    qb = q_ref[...].astype(F32)
    tiles = [qb[:, k * LANES:(k + 1) * LANES] for k in range(GC_ROPE_W // LANES)]
    swap = lambda t: pltpu.roll(t, HEAD_DIM, 1)
    keep_lo = lambda t: jnp.where(lo_half, t, 0.0)
    keep_hi = lambda t: jnp.where(lo_half, 0.0, t)
    r_att = jnp.concatenate(
        [keep_lo(tiles[0]), keep_lo(swap(tiles[0])), keep_lo(tiles[1]), keep_lo(swap(tiles[1])),
         keep_lo(tiles[2])], axis=0).astype(BF16)
    r_idx = jnp.concatenate(
        [keep_hi(tiles[2]), keep_hi(swap(tiles[3])), keep_hi(tiles[3]), keep_hi(swap(tiles[4]))],
        axis=0).astype(BF16)
    wt = wt_ref[...]
    q_chunk = (i * QB_DSA + lane) // CHUNK
    k_chunk0 = lax.broadcasted_iota(I32, (KB_DSA, 1), 0) // CHUNK

    def kslice(jb):
        return slice(jb * KB_DSA, (jb + 1) * KB_DSA)

    def score_block(jb):
        raw = _dot_nt(kk_ref[kslice(jb), :], r_idx)
        sc = jnp.zeros((KB_DSA, LANES), F32)
        for h in range(N_IDX_HEADS):
            sc = sc + jnp.maximum(raw[:, h * LANES:(h + 1) * LANES], 0.0) * wt[h:h + 1, :]
        adm = (k_chunk0 + jb * (KB_DSA // CHUNK)) <= q_chunk
        sc = jnp.where(adm, sc, -jnp.inf)
        bits = lax.bitcast_convert_type(sc, I32)
        key = jnp.where(bits < 0, bits ^ 0x7FFFFFFF, bits)
        key_ref[kslice(jb), :] = jnp.where(sc == 0.0, 0, key)

    for jb in range(n_kb):
        score_block(jb)

    def count_ge(cand):
        acc = jnp.zeros((8, LANES), I32)
        for jb in range(n_kb):
            ge = (key_ref[kslice(jb), :] >= cand).astype(I32)
            acc = acc + jnp.sum(ge.reshape(KB_DSA // 8, 8, LANES), axis=0)
        return jnp.sum(acc, axis=0, keepdims=True)

    def bisect(it, lo):
        cand = lo + jnp.left_shift(jnp.int32(1), 31 - it)
        return jnp.where(count_ge(cand) >= k_sel, cand, lo)

    thr = lax.fori_loop(0, 32, bisect, jnp.full((1, LANES), INT_MIN, I32))
    need = (k_sel - count_ge(thr + 1)).astype(F32)
    neg_inf_key = jnp.int32(-8388608) ^ 0x7FFFFFFF
    before = (lax.broadcasted_iota(I32, (KB_DSA, KB_DSA), 1)
              < lax.broadcasted_iota(I32, (KB_DSA, KB_DSA), 0)).astype(BF16)

    def scores(jb, ties):
        keyb = key_ref[kslice(jb), :]
        eq = keyb == thr
        eqf = jnp.where(eq, 1.0, 0.0)
        rank = _dot(before, eqf.astype(BF16)) + ties
        sel = ((keyb > thr) | (eq & (rank < need))) & (keyb > neg_inf_key)
        bias = jnp.where(sel, 0.0, NEG_BIG)
        st = _dot_nt(kk_ref[kslice(jb), :], r_att)
        st = st + jnp.concatenate([bias] * N_HEADS_DSA, axis=1)
        return st, ties + jnp.sum(eqf, axis=0, keepdims=True)

    def softmax_step(st, m, l):
        m_new = jnp.maximum(m, jnp.max(st, axis=0, keepdims=True))
        alpha = jnp.exp(m - m_new)
        p = jnp.exp(st - m_new)
        return m_new, alpha * l + jnp.sum(p, axis=0, keepdims=True), (alpha, p.astype(BF16))

    def values(jb, pend, acc):
        alpha, p = pend
        return alpha * acc + _dot(vt_ref[:, kslice(jb)], p)

    wide = N_HEADS_DSA * LANES

    ties = jnp.zeros((1, LANES), F32)
    m = jnp.full((1, wide), NEG_BIG, F32)
    l = jnp.zeros((1, wide), F32)
    acc = jnp.zeros((HEAD_DIM, wide), F32)
    for jb in range(n_kb):
        st, ties = scores(jb, ties)
        m, l, pend = softmax_step(st, m, l)
        acc = values(jb, pend, acc)
    o = acc * (1.0 / l)
    heads = [o[:, h * LANES:(h + 1) * LANES] for h in range(N_HEADS_DSA)]
    heads.append(jnp.zeros_like(heads[0]))
    for g in range(W_PAIRS // PAIR_W):
        y_ref[:, g * LANES:(g + 1) * LANES] = jnp.concatenate(
            [heads[2 * g], heads[2 * g + 1]], axis=0).T.astype(BF16)


def _dsa(dq, dkk, vt, wt, batch, seq):
    nq = seq // QB_DSA
    per_call = KB_DSA // QB_DSA
    outs = []
    for n_kb in range(1, seq // KB_DSA + 1):
        first = per_call * (n_kb - 1)
        outs.append(pl.pallas_call(
            functools.partial(_dsa_kernel, n_kb=n_kb),
            grid=(batch, per_call),
            in_specs=[
                pl.BlockSpec((QB_DSA, GC_ROPE_W), lambda b, j, first=first: (b * nq + first + j, 0)),
                pl.BlockSpec((seq, LANES), lambda b, j: (b, 0)),
                pl.BlockSpec((None, HEAD_DIM, seq), lambda b, j: (b, 0, 0)),
                pl.BlockSpec((None, 8, QB_DSA), lambda b, j, first=first: (b, 0, first + j)),
            ],
            out_specs=pl.BlockSpec((QB_DSA, W_PAIRS), lambda b, j: (b * per_call + j, 0)),
            out_shape=jax.ShapeDtypeStruct((batch * KB_DSA, W_PAIRS), BF16),
            scratch_shapes=[pltpu.VMEM((n_kb * KB_DSA, LANES), I32)],
            compiler_params=_cparams(("arbitrary", "arbitrary")),
            name=f"dsa_attn_{n_kb}",
        )(dq, dkk, vt, wt))
    y = jnp.stack([o.reshape(batch, KB_DSA, W_PAIRS) for o in outs], axis=1)
    return y.reshape(batch * seq, W_PAIRS)


def _merge_kernel(x_ref, ya_ref, yb_ref, yc_ref, gt_ref, wa_ref, wb_ref, wc_ref, wo_ref, o_ref):
    merged = None
    for k, (y_ref, w_ref) in enumerate(((ya_ref, wa_ref), (yb_ref, wb_ref), (yc_ref, wc_ref))):
        gate = gt_ref[:, k * D_MODEL:(k + 1) * D_MODEL].astype(F32)
        term = gate * _dot(y_ref[...], w_ref[...])
        merged = term if merged is None else merged + term
    o_ref[...] = x_ref[...] + _dot(merged.astype(BF16), wo_ref[...])


def _merge(x, ya, yb, yc, gates, wa, wb, wc, wo):
    m = x.shape[0]
    row = lambda width: pl.BlockSpec((TM_MERGE, width), lambda i: (i, 0))
    return pl.pallas_call(
        _merge_kernel,
        grid=(m // TM_MERGE,),
        in_specs=[row(D_MODEL), row(W_PAIRS), row(W_PAIRS), row(W_PAIRS), row(GG_W),
                  _resident(wa.shape), _resident(wb.shape), _resident(wc.shape), _resident(wo.shape)],
        out_specs=row(D_MODEL),
        out_shape=jax.ShapeDtypeStruct((m, D_MODEL), F32),
        compiler_params=_cparams(("arbitrary",)),
        name="merge_out",
    )(x, ya, yb, yc, gates, wa, wb, wc, wo)


def _pad_heads(w, axis):
    pad = [(0, 0)] * w.ndim
    pad[axis] = (0, W_PAIRS - w.shape[axis])
    return jnp.pad(w, pad)


def _layout_w_in(w_in):
    sizes = [W_FOX, W_FOX, W_FOX, N_HEADS_FOX, W_SB, W_SB, W_SB, W_DSA, KV_LATENT,
             N_IDX_HEADS * IDX_DIM, IDX_DIM, N_IDX_HEADS, 3 * D_MODEL]
    pieces, acc = [], 0
    for s in sizes:
        pieces.append(w_in[:, acc:acc + s])
        acc += s
    q_a, k_a, v_a, f_a, q_b, k_b, v_b, q_c, c_kv, q_i, k_i, w_i, gates = pieces
    scale = HEAD_DIM ** -0.5
    zeros = lambda n: jnp.zeros((D_MODEL, n), w_in.dtype)
    cols = [q_a * scale, k_a, v_a,
            _pad_heads(q_b * scale, 1), _pad_heads(k_b, 1), _pad_heads(v_b, 1),
            q_c * scale, q_i, k_i, c_kv,
            f_a, zeros(8 - N_HEADS_FOX), w_i, zeros(GS_W - 8 - N_IDX_HEADS),
            gates]
    out = jnp.concatenate(cols, axis=1).astype(BF16)
    assert out.shape == (D_MODEL, D_IN_PAD)
    return out


def _layout_ffn(w_gu, w_down):
    wg = w_gu[:, :D_FF].reshape(D_MODEL, N_FF_CHUNKS, FF_CHUNK)
    wu = w_gu[:, D_FF:].reshape(D_MODEL, N_FF_CHUNKS, FF_CHUNK)
    w1 = jnp.concatenate([wg, wu], axis=2).transpose(1, 0, 2).astype(BF16)
    w2 = w_down.reshape(N_FF_CHUNKS, FF_CHUNK, D_MODEL).astype(BF16)
    return w1, w2


def _rope_tables(seq):
    half = ROPE_DIM // 2
    pos = jnp.arange(seq, dtype=F32)
    inv = ROPE_THETA ** (-jnp.arange(0, ROPE_DIM, 2, dtype=F32) / ROPE_DIM)
    ang = pos[:, None] * inv[None, :]
    cos, sin = jnp.cos(ang), jnp.sin(ang)
    ones = jnp.ones((seq, HEAD_DIM - ROPE_DIM), F32)
    zeros_rest = jnp.zeros((seq, HEAD_DIM - ROPE_DIM), F32)
    zeros_half = jnp.zeros((seq, half), F32)
    c = jnp.concatenate([cos, cos, ones], axis=1)
    s1 = jnp.concatenate([-sin, zeros_half, zeros_rest], axis=1)
    s2 = jnp.concatenate([zeros_half, sin, zeros_rest], axis=1)
    two = lambda t: jnp.concatenate([t, t], axis=1)
    return two(c), two(s1), two(s2)


@jax.jit
def kernel(x, g_ffn1, w_ffn1_gu, w_ffn1_down, g_mix, w_in, b_forget, g_kv_latent, w_kv_up, g_idx_k,
           w_up_fox, w_up_sb, w_up_dsa, w_out, g_ffn2, w_ffn2_gu, w_ffn2_down, g_final):
    batch, seq, _ = x.shape
    depth = g_ffn1.shape[0]
    rc, rs1, rs2 = _rope_tables(seq)
    xf = x.reshape(batch * seq, D_MODEL)
    row = lambda v: v.reshape(1, -1).astype(F32)
    gf = row(g_final)
    for layer in range(depth):
        w1, w2 = _layout_ffn(w_ffn1_gu[layer], w_ffn1_down[layer])
        xf = _ffn(xf, row(g_ffn1[layer]), w1, w2, gf, False)

        gki = jnp.concatenate([jnp.ones((IDX_DIM,), F32), g_idx_k[layer]]).reshape(1, LANES)
        qk_a, qk_b, vt_a, vt_b, dq, dkk, vt, small, gates = _proj(
            xf, row(g_mix[layer]), _layout_w_in(w_in[layer]), w_kv_up[layer].astype(BF16),
            row(g_kv_latent[layer]), gki, rc, rs1, rs2, batch, seq)
        bcol = jnp.zeros((LANES, 1), F32).at[:N_HEADS_FOX, 0].set(b_forget[layer])
        c, ct, wt = _gate(small, bcol, batch, seq)
        y_a = _fox(qk_a, vt_a, c, ct, batch, seq)
        y_b = _sb(qk_b, vt_b, batch, seq)
        y_c = _dsa(dq, dkk, vt, wt, batch, seq)
        xf = _merge(xf, y_a, y_b, y_c, gates,
                    w_up_fox[layer].astype(BF16), _pad_heads(w_up_sb[layer], 0).astype(BF16),
                    _pad_heads(w_up_dsa[layer], 0).astype(BF16), w_out[layer].astype(BF16))

        w1, w2 = _layout_ffn(w_ffn2_gu[layer], w_ffn2_down[layer])
        xf = _ffn(xf, row(g_ffn2[layer]), w1, w2, gf, layer == depth - 1)
    return xf.reshape(batch, seq, D_MODEL)
```

```python
import functools

import jax
import jax.numpy as jnp
from jax import lax
from jax.experimental import pallas as pl
from jax.experimental.pallas import tpu as pltpu

F32 = jnp.float32
BF16 = jnp.bfloat16
I32 = jnp.int32

D_MODEL = 1024
HEAD_DIM = 64
N_HEADS_FOX = 6
N_HEADS_SB = 5
N_HEADS_DSA = 5
W_FOX = N_HEADS_FOX * HEAD_DIM
W_SB = N_HEADS_SB * HEAD_DIM
W_DSA = N_HEADS_DSA * HEAD_DIM
KV_LATENT = 128
N_IDX_HEADS = 4
IDX_DIM = 64
CHUNK = 64
TOPK_MAX = 256
ROPE_THETA = 500000.0
ROPE_DIM = HEAD_DIM // 4
D_FF = 2816
EPS = 1e-6

LANES = 128
PAIR_W = 2 * HEAD_DIM
W_PAIRS = 3 * PAIR_W
VMEM_LIMIT = 56 * 1024 * 1024

GA_W = 3 * W_FOX
GB_W = 3 * W_PAIRS
GC_ROPE_W = W_DSA + N_IDX_HEADS * IDX_DIM + IDX_DIM
GC_W = GC_ROPE_W + KV_LATENT
GS_W = LANES
GG_W = 3 * D_MODEL
GA_0 = 0
GB_0 = GA_0 + GA_W
GC_0 = GB_0 + GB_W
GS_0 = GC_0 + GC_W
GG_0 = GS_0 + GS_W
D_IN_PAD = GG_0 + GG_W

FF_CHUNK = 256
N_FF_CHUNKS = D_FF // FF_CHUNK

TM_FFN = 512
TM_PROJ = 512
TM_MERGE = 512
TQ = 256
QB_DSA = 128
KB_DSA = 256
NEG_BIG = -1e30
INT_MIN = -2147483648


def _cparams(sem):
    return pltpu.CompilerParams(dimension_semantics=sem, vmem_limit_bytes=VMEM_LIMIT)


def _resident(shape):
    nd = len(shape)
    return pl.BlockSpec(shape, lambda *_: (0,) * nd, pipeline_mode=pl.Buffered(1))


def _rms(x, g):
    ms = jnp.mean(x * x, axis=-1, keepdims=True)
    return x * lax.rsqrt(ms + EPS) * g


def _dot(a, b):
    return jnp.dot(a, b, preferred_element_type=F32)


def _dot_nt(a, b):
    return lax.dot_general(a, b, (((1,), (1,)), ((), ())), preferred_element_type=F32)


def _log_sigmoid(z):
    return jnp.minimum(z, 0.0) - jnp.log1p(jnp.exp(-jnp.abs(z)))


def _ffn_kernel(x_ref, g_ref, w1_ref, w2_ref, gf_ref, o_ref, acc_ref, *, final_norm):
    x = x_ref[...]
    h = _rms(x, g_ref[...]).astype(BF16)
    for j in range(N_FF_CHUNKS):
        gu = _dot(h, w1_ref[j])
        g = gu[:, :FF_CHUNK]
        u = gu[:, FF_CHUNK:]
        a = (g * jax.nn.sigmoid(g) * u).astype(BF16)
        d = _dot(a, w2_ref[j])
        if j == 0:
            acc_ref[...] = d
        else:
            acc_ref[...] += d
    y = x + 0.5 * acc_ref[...]
    if final_norm:
        y = _rms(y, gf_ref[...])
    o_ref[...] = y


def _ffn(x, g, w1, w2, gf, final_norm):
    m = x.shape[0]
    return pl.pallas_call(
        functools.partial(_ffn_kernel, final_norm=final_norm),
        grid=(m // TM_FFN,),
        in_specs=[
            pl.BlockSpec((TM_FFN, D_MODEL), lambda i: (i, 0)),
            _resident((1, D_MODEL)),
            _resident(w1.shape),
            _resident(w2.shape),
            _resident((1, D_MODEL)),
        ],
        out_specs=pl.BlockSpec((TM_FFN, D_MODEL), lambda i: (i, 0)),
        out_shape=jax.ShapeDtypeStruct((m, D_MODEL), F32),
        scratch_shapes=[pltpu.VMEM((TM_FFN, D_MODEL), F32)],
        compiler_params=_cparams(("arbitrary",)),
        name="ffn",
    )(x, g, w1, w2, gf)


def _rope_tile(t, c, s1, s2):
    return t * c + pltpu.roll(t, LANES - ROPE_DIM // 2, 1) * s1 + pltpu.roll(t, ROPE_DIM // 2, 1) * s2


def _proj_kernel(x_ref, g_ref, w_ref, wkv_ref, gkv_ref, gki_ref, rc_ref, rs1_ref, rs2_ref,
                 qa_ref, qb_ref, vat_ref, vbt_ref, dq_ref, dkk_ref, vt_ref, sm_ref, gt_ref):
    h = _rms(x_ref[...], g_ref[...]).astype(BF16)

    def proj(c0, width):
        return _dot(h, w_ref[:, c0:c0 + width])

    for g0, qk_ref, v_ref in ((GA_0, qa_ref, vat_ref), (GB_0, qb_ref, vbt_ref)):
        for c0 in range(0, 2 * W_PAIRS, W_PAIRS):
            qk_ref[:, c0:c0 + W_PAIRS] = proj(g0 + c0, W_PAIRS).astype(BF16)
        v = proj(g0 + 2 * W_PAIRS, W_PAIRS)
        for k in range(W_PAIRS // LANES):
            v_ref[k * LANES:(k + 1) * LANES, :] = v[:, k * LANES:(k + 1) * LANES].T.astype(BF16)
    sm_ref[...] = proj(GS_0, GS_W)
    for c0 in range(0, GG_W, 512):
        gt_ref[:, c0:c0 + 512] = jax.nn.sigmoid(proj(GG_0 + c0, 512)).astype(BF16)

    r = proj(GC_0, GC_W)
    lane = lax.broadcasted_iota(I32, (1, LANES), 1)
    hi_half = lane >= HEAD_DIM
    rc, rs1, rs2 = rc_ref[...], rs1_ref[...], rs2_ref[...]
    n_tiles = GC_ROPE_W // LANES
    last = None
    for k in range(n_tiles):
        t = r[:, k * LANES:(k + 1) * LANES]
        if k == n_tiles - 1:
            ms = jnp.sum(jnp.where(hi_half, t * t, 0.0), axis=-1, keepdims=True) * (1.0 / IDX_DIM)
            t = t * jnp.where(hi_half, lax.rsqrt(ms + EPS) * gki_ref[...], 1.0)
        t = _rope_tile(t, rc, rs1, rs2)
        dq_ref[:, k * LANES:(k + 1) * LANES] = t.astype(BF16)
        last = t
    ckv = _rms(r[:, GC_ROPE_W:GC_W], gkv_ref[...]).astype(BF16)
    kv = _dot(ckv, wkv_ref[...])
    lo_half = jnp.logical_not(hi_half)
    kv = _rope_tile(kv, jnp.where(lo_half, rc, 1.0), jnp.where(lo_half, rs1, 0.0),
                    jnp.where(lo_half, rs2, 0.0))
    dkk_ref[...] = jnp.where(lo_half, kv, last).astype(BF16)
    vt_ref[...] = kv.T[HEAD_DIM:, :].astype(BF16)


def _proj(x, g, w, wkv, gkv, gki, rc, rs1, rs2, batch, seq):
    m = x.shape[0]
    spb = seq // TM_PROJ
    row = lambda width: pl.BlockSpec((TM_PROJ, width), lambda i: (i, 0))
    pos = pl.BlockSpec((TM_PROJ, LANES), lambda i: (i % spb, 0))
    return pl.pallas_call(
        _proj_kernel,
        grid=(m // TM_PROJ,),
        in_specs=[
            row(D_MODEL), _resident((1, D_MODEL)), _resident(w.shape), _resident(wkv.shape),
            _resident((1, KV_LATENT)), _resident((1, LANES)), pos, pos, pos,
        ],
        out_specs=[
            row(2 * W_PAIRS), row(2 * W_PAIRS),
            pl.BlockSpec((None, W_PAIRS, TM_PROJ), lambda i: (i // spb, 0, i % spb)),
            pl.BlockSpec((None, W_PAIRS, TM_PROJ), lambda i: (i // spb, 0, i % spb)),
            row(GC_ROPE_W), row(LANES),
            pl.BlockSpec((None, HEAD_DIM, TM_PROJ), lambda i: (i // spb, 0, i % spb)),
            row(GS_W), row(GG_W),
        ],
        out_shape=[
            jax.ShapeDtypeStruct((m, 2 * W_PAIRS), BF16),
            jax.ShapeDtypeStruct((m, 2 * W_PAIRS), BF16),
            jax.ShapeDtypeStruct((batch, W_PAIRS, seq), BF16),
            jax.ShapeDtypeStruct((batch, W_PAIRS, seq), BF16),
            jax.ShapeDtypeStruct((m, GC_ROPE_W), BF16),
            jax.ShapeDtypeStruct((m, LANES), BF16),
            jax.ShapeDtypeStruct((batch, HEAD_DIM, seq), BF16),
            jax.ShapeDtypeStruct((m, GS_W), F32),
            jax.ShapeDtypeStruct((m, GG_W), BF16),
        ],
        compiler_params=_cparams(("arbitrary",)),
        name="in_proj",
    )(x, g, w, wkv, gkv, gki, rc, rs1, rs2)


def _gate_kernel(s_ref, b_ref, c_ref, ct_ref, wt_ref):
    xt = s_ref[...].T
    c = _log_sigmoid(xt + b_ref[...])
    seq = c.shape[1]
    lane = lax.broadcasted_iota(I32, c.shape, 1)
    d = 1
    while d < seq:
        c = c + jnp.where(lane >= d, pltpu.roll(c, d, 1), 0.0)
        d *= 2
    ct_ref[...] = c[0:8]
    wt_ref[...] = xt[8:16]
    c_ref[...] = c.T


def _gate(small, bcol, batch, seq):
    return pl.pallas_call(
        _gate_kernel,
        grid=(batch,),
        in_specs=[pl.BlockSpec((seq, LANES), lambda b: (b, 0)), _resident((LANES, 1))],
        out_specs=[
            pl.BlockSpec((seq, LANES), lambda b: (b, 0)),
            pl.BlockSpec((None, 8, seq), lambda b: (b, 0, 0)),
            pl.BlockSpec((None, 8, seq), lambda b: (b, 0, 0)),
        ],
        out_shape=[
            jax.ShapeDtypeStruct((batch * seq, LANES), F32),
            jax.ShapeDtypeStruct((batch, 8, seq), F32),
            jax.ShapeDtypeStruct((batch, 8, seq), F32),
        ],
        compiler_params=_cparams(("arbitrary",)),
        name="gate_cumsum",
    )(small, bcol)


def _rows(blk):
    return pl.ds(pl.multiple_of(blk * TQ, TQ), TQ)


def _pipelined_blocks(i, heads, qk, vpu, pv, state0, acc0):
    n = len(heads)
    s_diag = [qk(h, i) for h in heads]
    s_next = tuple(qk(h, jnp.maximum(i - 1, 0)) for h in heads)
    first = [vpu(heads[k], i, s_diag[k], state0[k], True) for k in range(n)]
    state = tuple(f[0] for f in first)
    pend = tuple(f[1] for f in first)

    def body(t, carry):
        s_cur, state, pend, acc = carry
        blk = i - t
        s_nxt = tuple(qk(h, jnp.maximum(blk - 1, 0)) for h in heads)
        acc = tuple(pv(heads[k], blk + 1, pend[k], acc[k]) for k in range(n))
        new = [vpu(heads[k], blk, s_cur[k], state[k], False) for k in range(n)]
        return s_nxt, tuple(x[0] for x in new), tuple(x[1] for x in new), acc

    _, state, pend, acc = lax.fori_loop(1, i + 1, body, (s_next, state, pend, tuple(acc0)))
    acc = tuple(pv(heads[k], 0, pend[k], acc[k]) for k in range(n))
    return state, acc


def _pair_masks():
    lane = lax.broadcasted_iota(I32, (1, LANES), 1)
    sub = lax.broadcasted_iota(I32, (LANES, 1), 0)
    return (lane < HEAD_DIM, lane >= HEAD_DIM), sub < HEAD_DIM


def _fox_kernel(q_ref, k_ref, vt_ref, c_ref, ct_ref, o_ref):
    i = pl.program_id(1)
    halves, sub_lo = _pair_masks()
    causal_t = (lax.broadcasted_iota(I32, (TQ, TQ), 0) <= lax.broadcasted_iota(I32, (TQ, TQ), 1))
    heads = tuple(range(N_HEADS_FOX))
    pair_cols = lambda h: slice((h // 2) * LANES, (h // 2 + 1) * LANES)
    qm = []
    for h in heads:
        q = q_ref[:, pair_cols(h)]
        qm.append(jnp.where(halves[h % 2], q, jnp.zeros_like(q)))
    ct_q = ct_ref[:, _rows(i)]

    def qk(h, blk):
        return _dot_nt(k_ref[_rows(blk), pair_cols(h)], qm[h])

    def vpu(h, blk, s, state, masked):
        m, l = state
        cq = ct_q[h:h + 1, :]
        u = s - c_ref[_rows(blk), h:h + 1]
        if masked:
            u = jnp.where(causal_t, u, -jnp.inf)
        m_new = jnp.maximum(m, jnp.max(u, axis=0, keepdims=True) + cq)
        alpha = jnp.exp(m - m_new)
        p = jnp.exp(u + (cq - m_new))
        l = alpha * l + jnp.sum(p, axis=0, keepdims=True)
        return (m_new, l), (alpha, p.astype(BF16))

    def pv(h, blk, pend, acc):
        alpha, p = pend
        return alpha * acc + _dot(vt_ref[pair_cols(h), _rows(blk)], p)

    n = len(heads)
    state0 = [(jnp.full((1, TQ), -jnp.inf, F32), jnp.zeros((1, TQ), F32))] * n
    acc0 = [jnp.zeros((LANES, TQ), F32)] * n
    state, acc = _pipelined_blocks(i, heads, qk, vpu, pv, state0, acc0)
    outs = [acc[h] * (1.0 / state[h][1]) for h in heads]
    for g in range(n // 2):
        o_ref[:, g * LANES:(g + 1) * LANES] = jnp.where(sub_lo, outs[2 * g], outs[2 * g + 1]).T.astype(BF16)


def _fox(qkv, vt, c, ct, batch, seq):
    nq = seq // TQ
    return pl.pallas_call(
        _fox_kernel,
        grid=(batch, nq),
        in_specs=[
            pl.BlockSpec((TQ, W_FOX), lambda b, i: (b * nq + i, 0)),
            pl.BlockSpec((seq, W_FOX), lambda b, i: (b, 1)),
            pl.BlockSpec((None, W_FOX, seq), lambda b, i: (b, 0, 0)),
            pl.BlockSpec((seq, LANES), lambda b, i: (b, 0)),
            pl.BlockSpec((None, 8, seq), lambda b, i: (b, 0, 0)),
        ],
        out_specs=pl.BlockSpec((TQ, W_FOX), lambda b, i: (b * nq + i, 0)),
        out_shape=jax.ShapeDtypeStruct((batch * seq, W_FOX), BF16),
        compiler_params=_cparams(("arbitrary", "arbitrary")),
        name="fox_attn",
    )(qkv, qkv, vt, c, ct)


def _sb_kernel(q_ref, k_ref, vt_ref, o_ref):
    i = pl.program_id(1)
    halves, sub_lo = _pair_masks()
    row = lax.broadcasted_iota(I32, (TQ, TQ), 0)
    col = lax.broadcasted_iota(I32, (TQ, TQ), 1)
    strict_t = row < col
    half_blk = TQ // 2
    tri = (col >= row)[:half_blk, :half_blk].astype(BF16)
    tri2 = jnp.concatenate([tri, tri], axis=1)
    heads = tuple(range(N_HEADS_SB))
    pair_cols = lambda h: slice((h // 2) * LANES, (h // 2 + 1) * LANES)
    qm = []
    for h in heads:
        q = q_ref[:, pair_cols(h)]
        qm.append(jnp.where(halves[h % 2], q, jnp.zeros_like(q)))

    def qk(h, blk):
        return _dot_nt(k_ref[_rows(blk), pair_cols(h)], qm[h])

    def vpu(h, blk, z, tail, masked):
        nz = -z
        l1m = jnp.minimum(nz, 0.0) - jnp.log(1.0 + jnp.exp(jnp.minimum(z, nz)))
        if masked:
            l1m = jnp.where(strict_t, l1m, 0.0)
        hi = l1m.astype(BF16)
        lo = (l1m - hi.astype(F32)).astype(BF16)
        sum_a = jnp.sum(l1m[:half_blk], axis=0, keepdims=True)
        sum_b = jnp.sum(l1m[half_blk:], axis=0, keepdims=True)
        run_a = _dot(tri2, jnp.concatenate([hi[:half_blk], lo[:half_blk]], axis=0)) + sum_b
        run_b = _dot(tri2, jnp.concatenate([hi[half_blk:], lo[half_blk:]], axis=0))
        a = jnp.exp(z + jnp.concatenate([run_a, run_b], axis=0) + tail)
        if masked:
            a = jnp.where(strict_t, a, 0.0)
        return tail + (sum_a + sum_b), a.astype(BF16)

    def pv(h, blk, a, acc):
        return acc + _dot(vt_ref[pair_cols(h), _rows(blk)], a)

    n = len(heads)
    _, acc = _pipelined_blocks(i, heads, qk, vpu, pv, [jnp.zeros((1, TQ), F32)] * n,
                               [jnp.zeros((LANES, TQ), F32)] * n)
    acc = list(acc) + [jnp.zeros((LANES, TQ), F32)]
    for g in range(W_PAIRS // PAIR_W):
        o_ref[:, g * LANES:(g + 1) * LANES] = jnp.where(sub_lo, acc[2 * g], acc[2 * g + 1]).T.astype(BF16)


def _sb(qkv, vt, batch, seq):
    nq = seq // TQ
    return pl.pallas_call(
        _sb_kernel,
        grid=(batch, nq),
        in_specs=[
            pl.BlockSpec((TQ, W_PAIRS), lambda b, i: (b * nq + i, 0)),
            pl.BlockSpec((seq, W_PAIRS), lambda b, i: (b, 1)),
            pl.BlockSpec((None, W_PAIRS, seq), lambda b, i: (b, 0, 0)),
        ],
        out_specs=pl.BlockSpec((TQ, W_PAIRS), lambda b, i: (b * nq + i, 0)),
        out_shape=jax.ShapeDtypeStruct((batch * seq, W_PAIRS), BF16),
        compiler_params=_cparams(("arbitrary", "arbitrary")),
        name="sb_attn",
    )(qkv, qkv, vt)


def _dsa_kernel(q_ref, kk_ref, vt_ref, wt_ref, y_ref, key_ref, *, n_kb):
    i = 2 * (n_kb - 1) + pl.program_id(1)
    k_sel = jnp.minimum(TOPK_MAX, QB_DSA * (i + 1))
    lane = lax.broadcasted_iota(I32, (1, LANES), 1)
    lo_half = lane < HEAD_DIM

    qb = q_ref[...].astype(F32)
    tiles = [qb[:, k * LANES:(k + 1) * LANES] for k in range(GC_ROPE_W // LANES)]
    swap = lambda t: pltpu.roll(t, HEAD_DIM, 1)
    keep_lo = lambda t: jnp.where(lo_half, t, 0.0)
    keep_hi = lambda t: jnp.where(lo_half, 0.0, t)
    r_att = jnp.concatenate(
        [keep_lo(tiles[0]), keep_lo(swap(tiles[0])), keep_lo(tiles[1]), keep_lo(swap(tiles[1])),
         keep_lo(tiles[2])], axis=0).astype(BF16)
    r_idx = jnp.concatenate(
        [keep_hi(tiles[2]), keep_hi(swap(tiles[3])), keep_hi(tiles[3]), keep_hi(swap(tiles[4]))],
        axis=0).astype(BF16)
    wt = wt_ref[...]
    q_chunk = (i * QB_DSA + lane) // CHUNK
    k_chunk0 = lax.broadcasted_iota(I32, (KB_DSA, 1), 0) // CHUNK

    def kslice(jb):
        return slice(jb * KB_DSA, (jb + 1) * KB_DSA)

    def score_block(jb):
        raw = _dot_nt(kk_ref[kslice(jb), :], r_idx)
        sc = jnp.zeros((KB_DSA, LANES), F32)
        for h in range(N_IDX_HEADS):
            sc = sc + jnp.maximum(raw[:, h * LANES:(h + 1) * LANES], 0.0) * wt[h:h + 1, :]
        adm = (k_chunk0 + jb * (KB_DSA // CHUNK)) <= q_chunk
        sc = jnp.where(adm, sc, -jnp.inf)
        bits = lax.bitcast_convert_type(sc, I32)
        key = jnp.where(bits < 0, bits ^ 0x7FFFFFFF, bits)
        key_ref[kslice(jb), :] = jnp.where(sc == 0.0, 0, key)

    for jb in range(n_kb):
        score_block(jb)

    def count_ge(cand):
        acc = jnp.zeros((8, LANES), I32)
        for jb in range(n_kb):
            ge = (key_ref[kslice(jb), :] >= cand).astype(I32)
            acc = acc + jnp.sum(ge.reshape(KB_DSA // 8, 8, LANES), axis=0)
        return jnp.sum(acc, axis=0, keepdims=True)

    def bisect(it, lo):
        cand = lo + jnp.left_shift(jnp.int32(1), 31 - it)
        return jnp.where(count_ge(cand) >= k_sel, cand, lo)

    thr = lax.fori_loop(0, 32, bisect, jnp.full((1, LANES), INT_MIN, I32))
    need = (k_sel - count_ge(thr + 1)).astype(F32)
    neg_inf_key = jnp.int32(-8388608) ^ 0x7FFFFFFF
    before = (lax.broadcasted_iota(I32, (KB_DSA, KB_DSA), 1)
              < lax.broadcasted_iota(I32, (KB_DSA, KB_DSA), 0)).astype(BF16)

    def scores(jb, ties):
        keyb = key_ref[kslice(jb), :]
        eq = keyb == thr
        eqf = jnp.where(eq, 1.0, 0.0)
        rank = _dot(before, eqf.astype(BF16)) + ties
        sel = ((keyb > thr) | (eq & (rank < need))) & (keyb > neg_inf_key)
        bias = jnp.where(sel, 0.0, NEG_BIG)
        st = _dot_nt(kk_ref[kslice(jb), :], r_att)
        st = st + jnp.concatenate([bias] * N_HEADS_DSA, axis=1)
        return st, ties + jnp.sum(eqf, axis=0, keepdims=True)

    def softmax_step(st, m, l):
        m_new = jnp.maximum(m, jnp.max(st, axis=0, keepdims=True))
        alpha = jnp.exp(m - m_new)
        p = jnp.exp(st - m_new)
        return m_new, alpha * l + jnp.sum(p, axis=0, keepdims=True), (alpha, p.astype(BF16))

    def values(jb, pend, acc):
        alpha, p = pend
        return alpha * acc + _dot(vt_ref[:, kslice(jb)], p)

    wide = N_HEADS_DSA * LANES

    ties = jnp.zeros((1, LANES), F32)
    m = jnp.full((1, wide), NEG_BIG, F32)
    l = jnp.zeros((1, wide), F32)
    acc = jnp.zeros((HEAD_DIM, wide), F32)
    for jb in range(n_kb):
        st, ties = scores(jb, ties)
        m, l, pend = softmax_step(st, m, l)
        acc = values(jb, pend, acc)
    o = acc * (1.0 / l)
    heads = [o[:, h * LANES:(h + 1) * LANES] for h in range(N_HEADS_DSA)]
    heads.append(jnp.zeros_like(heads[0]))
    for g in range(W_PAIRS // PAIR_W):
        y_ref[:, g * LANES:(g + 1) * LANES] = jnp.concatenate(
            [heads[2 * g], heads[2 * g + 1]], axis=0).T.astype(BF16)


def _dsa(dq, dkk, vt, wt, batch, seq):
    nq = seq // QB_DSA
    per_call = KB_DSA // QB_DSA
    outs = []
    for n_kb in range(1, seq // KB_DSA + 1):
        first = per_call * (n_kb - 1)
        outs.append(pl.pallas_call(
            functools.partial(_dsa_kernel, n_kb=n_kb),
            grid=(batch, per_call),
            in_specs=[
                pl.BlockSpec((QB_DSA, GC_ROPE_W), lambda b, j, first=first: (b * nq + first + j, 0)),
                pl.BlockSpec((seq, LANES), lambda b, j: (b, 0)),
                pl.BlockSpec((None, HEAD_DIM, seq), lambda b, j: (b, 0, 0)),
                pl.BlockSpec((None, 8, QB_DSA), lambda b, j, first=first: (b, 0, first + j)),
            ],
            out_specs=pl.BlockSpec((QB_DSA, W_PAIRS), lambda b, j: (b * per_call + j, 0)),
            out_shape=jax.ShapeDtypeStruct((batch * KB_DSA, W_PAIRS), BF16),
            scratch_shapes=[pltpu.VMEM((n_kb * KB_DSA, LANES), I32)],
            compiler_params=_cparams(("arbitrary", "arbitrary")),
            name=f"dsa_attn_{n_kb}",
        )(dq, dkk, vt, wt))
    y = jnp.stack([o.reshape(batch, KB_DSA, W_PAIRS) for o in outs], axis=1)
    return y.reshape(batch * seq, W_PAIRS)


def _merge_kernel(x_ref, ya_ref, yb_ref, yc_ref, gt_ref, wa_ref, wb_ref, wc_ref, wo_ref, o_ref):
    merged = None
    for k, (y_ref, w_ref) in enumerate(((ya_ref, wa_ref), (yb_ref, wb_ref), (yc_ref, wc_ref))):
        gate = gt_ref[:, k * D_MODEL:(k + 1) * D_MODEL].astype(F32)
        term = gate * _dot(y_ref[...], w_ref[...])
        merged = term if merged is None else merged + term
    o_ref[...] = x_ref[...] + _dot(merged.astype(BF16), wo_ref[...])


def _merge(x, ya, yb, yc, gates, wa, wb, wc, wo):
    m = x.shape[0]
    row = lambda width: pl.BlockSpec((TM_MERGE, width), lambda i: (i, 0))
    return pl.pallas_call(
        _merge_kernel,
        grid=(m // TM_MERGE,),
        in_specs=[row(D_MODEL), row(W_PAIRS), row(W_PAIRS), row(W_PAIRS), row(GG_W),
                  _resident(wa.shape), _resident(wb.shape), _resident(wc.shape), _resident(wo.shape)],
        out_specs=row(D_MODEL),
        out_shape=jax.ShapeDtypeStruct((m, D_MODEL), F32),
        compiler_params=_cparams(("arbitrary",)),
        name="merge_out",
    )(x, ya, yb, yc, gates, wa, wb, wc, wo)


def _pad_heads(w, axis):
    pad = [(0, 0)] * w.ndim
    pad[axis] = (0, W_PAIRS - w.shape[axis])
    return jnp.pad(w, pad)


def _layout_w_in(w_in):
    sizes = [W_FOX, W_FOX, W_FOX, N_HEADS_FOX, W_SB, W_SB, W_SB, W_DSA, KV_LATENT,
             N_IDX_HEADS * IDX_DIM, IDX_DIM, N_IDX_HEADS, 3 * D_MODEL]
    pieces, acc = [], 0
    for s in sizes:
        pieces.append(w_in[:, acc:acc + s])
        acc += s
    q_a, k_a, v_a, f_a, q_b, k_b, v_b, q_c, c_kv, q_i, k_i, w_i, gates = pieces
    scale = HEAD_DIM ** -0.5
    zeros = lambda n: jnp.zeros((D_MODEL, n), w_in.dtype)
    cols = [q_a * scale, k_a, v_a,
            _pad_heads(q_b * scale, 1), _pad_heads(k_b, 1), _pad_heads(v_b, 1),
            q_c * scale, q_i, k_i, c_kv,
            f_a, zeros(8 - N_HEADS_FOX), w_i, zeros(GS_W - 8 - N_IDX_HEADS),
            gates]
    out = jnp.concatenate(cols, axis=1).astype(BF16)
    assert out.shape == (D_MODEL, D_IN_PAD)
    return out


def _layout_ffn(w_gu, w_down):
    wg = w_gu[:, :D_FF].reshape(D_MODEL, N_FF_CHUNKS, FF_CHUNK)
    wu = w_gu[:, D_FF:].reshape(D_MODEL, N_FF_CHUNKS, FF_CHUNK)
    w1 = jnp.concatenate([wg, wu], axis=2).transpose(1, 0, 2).astype(BF16)
    w2 = w_down.reshape(N_FF_CHUNKS, FF_CHUNK, D_MODEL).astype(BF16)
    return w1, w2


def _rope_tables(seq):
    half = ROPE_DIM // 2
    pos = jnp.arange(seq, dtype=F32)
    inv = ROPE_THETA ** (-jnp.arange(0, ROPE_DIM, 2, dtype=F32) / ROPE_DIM)
    ang = pos[:, None] * inv[None, :]
    cos, sin = jnp.cos(ang), jnp.sin(ang)
    ones = jnp.ones((seq, HEAD_DIM - ROPE_DIM), F32)
    zeros_rest = jnp.zeros((seq, HEAD_DIM - ROPE_DIM), F32)
    zeros_half = jnp.zeros((seq, half), F32)
    c = jnp.concatenate([cos, cos, ones], axis=1)
    s1 = jnp.concatenate([-sin, zeros_half, zeros_rest], axis=1)
    s2 = jnp.concatenate([zeros_half, sin, zeros_rest], axis=1)
    two = lambda t: jnp.concatenate([t, t], axis=1)
    return two(c), two(s1), two(s2)


@jax.jit
def kernel(x, g_ffn1, w_ffn1_gu, w_ffn1_down, g_mix, w_in, b_forget, g_kv_latent, w_kv_up, g_idx_k,
           w_up_fox, w_up_sb, w_up_dsa, w_out, g_ffn2, w_ffn2_gu, w_ffn2_down, g_final):
    batch, seq, _ = x.shape
    depth = g_ffn1.shape[0]
    rc, rs1, rs2 = _rope_tables(seq)
    xf = x.reshape(batch * seq, D_MODEL)
    row = lambda v: v.reshape(1, -1).astype(F32)
    gf = row(g_final)
    for layer in range(depth):
        w1, w2 = _layout_ffn(w_ffn1_gu[layer], w_ffn1_down[layer])
        xf = _ffn(xf, row(g_ffn1[layer]), w1, w2, gf, False)

        gki = jnp.concatenate([jnp.ones((IDX_DIM,), F32), g_idx_k[layer]]).reshape(1, LANES)
        qk_a, qk_b, vt_a, vt_b, dq, dkk, vt, small, gates = _proj(
            xf, row(g_mix[layer]), _layout_w_in(w_in[layer]), w_kv_up[layer].astype(BF16),
            row(g_kv_latent[layer]), gki, rc, rs1, rs2, batch, seq)
        bcol = jnp.zeros((LANES, 1), F32).at[:N_HEADS_FOX, 0].set(b_forget[layer])
        c, ct, wt = _gate(small, bcol, batch, seq)
        y_a = _fox(qk_a, vt_a, c, ct, batch, seq)
        y_b = _sb(qk_b, vt_b, batch, seq)
        y_c = _dsa(dq, dkk, vt, wt, batch, seq)
        xf = _merge(xf, y_a, y_b, y_c, gates,
                    w_up_fox[layer].astype(BF16), _pad_heads(w_up_sb[layer], 0).astype(BF16),
                    _pad_heads(w_up_dsa[layer], 0).astype(BF16), w_out[layer].astype(BF16))

        w1, w2 = _layout_ffn(w_ffn2_gu[layer], w_ffn2_down[layer])
        xf = _ffn(xf, row(g_ffn2[layer]), w1, w2, gf, layer == depth - 1)
    return xf.reshape(batch, seq, D_MODEL)
```

```python
import functools

import jax
import jax.numpy as jnp
from jax import lax
from jax.experimental import pallas as pl
from jax.experimental.pallas import tpu as pltpu

F32 = jnp.float32
BF16 = jnp.bfloat16
I32 = jnp.int32

D_MODEL = 1024
HEAD_DIM = 64
N_HEADS_FOX = 6
N_HEADS_SB = 5
N_HEADS_DSA = 5
W_FOX = N_HEADS_FOX * HEAD_DIM
W_SB = N_HEADS_SB * HEAD_DIM
W_DSA = N_HEADS_DSA * HEAD_DIM
KV_LATENT = 128
N_IDX_HEADS = 4
IDX_DIM = 64
CHUNK = 64
TOPK_MAX = 256
ROPE_THETA = 500000.0
ROPE_DIM = HEAD_DIM // 4
D_FF = 2816
EPS = 1e-6

LANES = 128
PAIR_W = 2 * HEAD_DIM
W_PAIRS = 3 * PAIR_W
VMEM_LIMIT = 56 * 1024 * 1024

GA_W = 3 * W_FOX
GB_W = 3 * W_PAIRS
GC_ROPE_W = W_DSA + N_IDX_HEADS * IDX_DIM + IDX_DIM
GC_W = GC_ROPE_W + KV_LATENT
GS_W = LANES
GG_W = 3 * D_MODEL
GA_0 = 0
GB_0 = GA_0 + GA_W
GC_0 = GB_0 + GB_W
GS_0 = GC_0 + GC_W
GG_0 = GS_0 + GS_W
D_IN_PAD = GG_0 + GG_W

FF_CHUNK = 256
N_FF_CHUNKS = D_FF // FF_CHUNK

TM_FFN = 512
TM_PROJ = 512
TM_MERGE = 512
TQ = 256
QB_DSA = 128
KB_DSA = 256
NEG_BIG = -1e30
INT_MIN = -2147483648


def _cparams(sem):
    return pltpu.CompilerParams(dimension_semantics=sem, vmem_limit_bytes=VMEM_LIMIT)


def _resident(shape):
    nd = len(shape)
    return pl.BlockSpec(shape, lambda *_: (0,) * nd, pipeline_mode=pl.Buffered(1))


def _rms(x, g):
    ms = jnp.mean(x * x, axis=-1, keepdims=True)
    return x * lax.rsqrt(ms + EPS) * g


def _dot(a, b):
    return jnp.dot(a, b, preferred_element_type=F32)


def _dot_nt(a, b):
    return lax.dot_general(a, b, (((1,), (1,)), ((), ())), preferred_element_type=F32)


def _log_sigmoid(z):
    return jnp.minimum(z, 0.0) - jnp.log1p(jnp.exp(-jnp.abs(z)))


def _ffn_kernel(x_ref, g_ref, w1_ref, w2_ref, gf_ref, o_ref, acc_ref, *, final_norm):
    x = x_ref[...]
    h = _rms(x, g_ref[...]).astype(BF16)
    for j in range(N_FF_CHUNKS):
        gu = _dot(h, w1_ref[j])
        g = gu[:, :FF_CHUNK]
        u = gu[:, FF_CHUNK:]
        a = (g * jax.nn.sigmoid(g) * u).astype(BF16)
        d = _dot(a, w2_ref[j])
        if j == 0:
            acc_ref[...] = d
        else:
            acc_ref[...] += d
    y = x + 0.5 * acc_ref[...]
    if final_norm:
        y = _rms(y, gf_ref[...])
    o_ref[...] = y


def _ffn(x, g, w1, w2, gf, final_norm):
    m = x.shape[0]
    return pl.pallas_call(
        functools.partial(_ffn_kernel, final_norm=final_norm),
        grid=(m // TM_FFN,),
        in_specs=[
            pl.BlockSpec((TM_FFN, D_MODEL), lambda i: (i, 0)),
            _resident((1, D_MODEL)),
            _resident(w1.shape),
            _resident(w2.shape),
            _resident((1, D_MODEL)),
        ],
        out_specs=pl.BlockSpec((TM_FFN, D_MODEL), lambda i: (i, 0)),
        out_shape=jax.ShapeDtypeStruct((m, D_MODEL), F32),
        scratch_shapes=[pltpu.VMEM((TM_FFN, D_MODEL), F32)],
        compiler_params=_cparams(("arbitrary",)),
        name="ffn",
    )(x, g, w1, w2, gf)


def _rope_tile(t, c, s1, s2):
    return t * c + pltpu.roll(t, LANES - ROPE_DIM // 2, 1) * s1 + pltpu.roll(t, ROPE_DIM // 2, 1) * s2


def _proj_kernel(x_ref, g_ref, w_ref, wkv_ref, gkv_ref, gki_ref, rc_ref, rs1_ref, rs2_ref,
                 qa_ref, qb_ref, vat_ref, vbt_ref, dq_ref, dkk_ref, vt_ref, sm_ref, gt_ref):
    h = _rms(x_ref[...], g_ref[...]).astype(BF16)

    def proj(c0, width):
        return _dot(h, w_ref[:, c0:c0 + width])

    for g0, qk_ref, v_ref in ((GA_0, qa_ref, vat_ref), (GB_0, qb_ref, vbt_ref)):
        for c0 in range(0, 2 * W_PAIRS, W_PAIRS):
            qk_ref[:, c0:c0 + W_PAIRS] = proj(g0 + c0, W_PAIRS).astype(BF16)
        v = proj(g0 + 2 * W_PAIRS, W_PAIRS)
        for k in range(W_PAIRS // LANES):
            v_ref[k * LANES:(k + 1) * LANES, :] = v[:, k * LANES:(k + 1) * LANES].T.astype(BF16)
    sm_ref[...] = proj(GS_0, GS_W)
    for c0 in range(0, GG_W, 512):
        gt_ref[:, c0:c0 + 512] = jax.nn.sigmoid(proj(GG_0 + c0, 512)).astype(BF16)

    r = proj(GC_0, GC_W)
    lane = lax.broadcasted_iota(I32, (1, LANES), 1)
    hi_half = lane >= HEAD_DIM
    rc, rs1, rs2 = rc_ref[...], rs1_ref[...], rs2_ref[...]
    n_tiles = GC_ROPE_W // LANES
    last = None
    for k in range(n_tiles):
        t = r[:, k * LANES:(k + 1) * LANES]
        if k == n_tiles - 1:
            ms = jnp.sum(jnp.where(hi_half, t * t, 0.0), axis=-1, keepdims=True) * (1.0 / IDX_DIM)
            t = t * jnp.where(hi_half, lax.rsqrt(ms + EPS) * gki_ref[...], 1.0)
        t = _rope_tile(t, rc, rs1, rs2)
        dq_ref[:, k * LANES:(k + 1) * LANES] = t.astype(BF16)
        last = t
    ckv = _rms(r[:, GC_ROPE_W:GC_W], gkv_ref[...]).astype(BF16)
    kv = _dot(ckv, wkv_ref[...])
    lo_half = jnp.logical_not(hi_half)
    kv = _rope_tile(kv, jnp.where(lo_half, rc, 1.0), jnp.where(lo_half, rs1, 0.0),
                    jnp.where(lo_half, rs2, 0.0))
    dkk_ref[...] = jnp.where(lo_half, kv, last).astype(BF16)
    vt_ref[...] = kv.T[HEAD_DIM:, :].astype(BF16)


def _proj(x, g, w, wkv, gkv, gki, rc, rs1, rs2, batch, seq):
    m = x.shape[0]
    spb = seq // TM_PROJ
    row = lambda width: pl.BlockSpec((TM_PROJ, width), lambda i: (i, 0))
    pos = pl.BlockSpec((TM_PROJ, LANES), lambda i: (i % spb, 0))
    return pl.pallas_call(
        _proj_kernel,
        grid=(m // TM_PROJ,),
        in_specs=[
            row(D_MODEL), _resident((1, D_MODEL)), _resident(w.shape), _resident(wkv.shape),
            _resident((1, KV_LATENT)), _resident((1, LANES)), pos, pos, pos,
        ],
        out_specs=[
            row(2 * W_PAIRS), row(2 * W_PAIRS),
            pl.BlockSpec((None, W_PAIRS, TM_PROJ), lambda i: (i // spb, 0, i % spb)),
            pl.BlockSpec((None, W_PAIRS, TM_PROJ), lambda i: (i // spb, 0, i % spb)),
            row(GC_ROPE_W), row(LANES),
            pl.BlockSpec((None, HEAD_DIM, TM_PROJ), lambda i: (i // spb, 0, i % spb)),
            row(GS_W), row(GG_W),
        ],
        out_shape=[
            jax.ShapeDtypeStruct((m, 2 * W_PAIRS), BF16),
            jax.ShapeDtypeStruct((m, 2 * W_PAIRS), BF16),
            jax.ShapeDtypeStruct((batch, W_PAIRS, seq), BF16),
            jax.ShapeDtypeStruct((batch, W_PAIRS, seq), BF16),
            jax.ShapeDtypeStruct((m, GC_ROPE_W), BF16),
            jax.ShapeDtypeStruct((m, LANES), BF16),
            jax.ShapeDtypeStruct((batch, HEAD_DIM, seq), BF16),
            jax.ShapeDtypeStruct((m, GS_W), F32),
            jax.ShapeDtypeStruct((m, GG_W), BF16),
        ],
        compiler_params=_cparams(("arbitrary",)),
        name="in_proj",
    )(x, g, w, wkv, gkv, gki, rc, rs1, rs2)


def _gate_kernel(s_ref, b_ref, c_ref, ct_ref, wt_ref):
    xt = s_ref[...].T
    c = _log_sigmoid(xt + b_ref[...])
    seq = c.shape[1]
    lane = lax.broadcasted_iota(I32, c.shape, 1)
    d = 1
    while d < seq:
        c = c + jnp.where(lane >= d, pltpu.roll(c, d, 1), 0.0)
        d *= 2
    ct_ref[...] = c[0:8]
    wt_ref[...] = xt[8:16]
    c_ref[...] = c.T


def _gate(small, bcol, batch, seq):
    return pl.pallas_call(
        _gate_kernel,
        grid=(batch,),
        in_specs=[pl.BlockSpec((seq, LANES), lambda b: (b, 0)), _resident((LANES, 1))],
        out_specs=[
            pl.BlockSpec((seq, LANES), lambda b: (b, 0)),
            pl.BlockSpec((None, 8, seq), lambda b: (b, 0, 0)),
            pl.BlockSpec((None, 8, seq), lambda b: (b, 0, 0)),
        ],
        out_shape=[
            jax.ShapeDtypeStruct((batch * seq, LANES), F32),
            jax.ShapeDtypeStruct((batch, 8, seq), F32),
            jax.ShapeDtypeStruct((batch, 8, seq), F32),
        ],
        compiler_params=_cparams(("arbitrary",)),
        name="gate_cumsum",
    )(small, bcol)


def _rows(blk):
    return pl.ds(pl.multiple_of(blk * TQ, TQ), TQ)


def _pipelined_blocks(i, heads, qk, vpu, pv, state0, acc0):
    n = len(heads)
    s_diag = [qk(h, i) for h in heads]
    s_next = tuple(qk(h, jnp.maximum(i - 1, 0)) for h in heads)
    first = [vpu(heads[k], i, s_diag[k], state0[k], True) for k in range(n)]
    state = tuple(f[0] for f in first)
    pend = tuple(f[1] for f in first)

    def step(t, carry):
        s_cur, state, pend, acc = carry
        blk = i - t
        s_nxt = tuple(qk(h, jnp.maximum(blk - 1, 0)) for h in heads)
        acc = tuple(pv(heads[k], blk + 1, pend[k], acc[k]) for k in range(n))
        new = [vpu(heads[k], blk, s_cur[k], state[k], False) for k in range(n)]
        return s_nxt, tuple(x[0] for x in new), tuple(x[1] for x in new), acc

    carry = lax.fori_loop(0, i // 2, lambda u, c: step(2 * u + 2, step(2 * u + 1, c)),
                          (s_next, state, pend, tuple(acc0)))
    _, state, pend, acc = lax.cond(i % 2 == 1, lambda c: step(i, c), lambda c: c, carry)
    acc = tuple(pv(heads[k], 0, pend[k], acc[k]) for k in range(n))
    return state, acc


def _pair_masks():
    lane = lax.broadcasted_iota(I32, (1, LANES), 1)
    sub = lax.broadcasted_iota(I32, (LANES, 1), 0)
    return (lane < HEAD_DIM, lane >= HEAD_DIM), sub < HEAD_DIM


def _fox_kernel(q_ref, k_ref, vt_ref, c_ref, ct_ref, o_ref):
    i = pl.program_id(1)
    halves, sub_lo = _pair_masks()
    causal_t = (lax.broadcasted_iota(I32, (TQ, TQ), 0) <= lax.broadcasted_iota(I32, (TQ, TQ), 1))
    heads = tuple(range(N_HEADS_FOX))
    pair_cols = lambda h: slice((h // 2) * LANES, (h // 2 + 1) * LANES)
    qm = []
    for h in heads:
        q = q_ref[:, pair_cols(h)]
        qm.append(jnp.where(halves[h % 2], q, jnp.zeros_like(q)))
    ct_q = ct_ref[:, _rows(i)]

    def qk(h, blk):
        return _dot_nt(k_ref[_rows(blk), pair_cols(h)], qm[h])

    def vpu(h, blk, s, state, masked):
        m, l = state
        cq = ct_q[h:h + 1, :]
        u = s - c_ref[_rows(blk), h:h + 1]
        if masked:
            u = jnp.where(causal_t, u, -jnp.inf)
        m_new = jnp.maximum(m, jnp.max(u, axis=0, keepdims=True) + cq)
        alpha = jnp.exp(m - m_new)
        p = jnp.exp(u + (cq - m_new))
        l = alpha * l + jnp.sum(p, axis=0, keepdims=True)
        return (m_new, l), (alpha, p.astype(BF16))

    def pv(h, blk, pend, acc):
        alpha, p = pend
        return alpha * acc + _dot(vt_ref[pair_cols(h), _rows(blk)], p)

    n = len(heads)
    state0 = [(jnp.full((1, TQ), -jnp.inf, F32), jnp.zeros((1, TQ), F32))] * n
    acc0 = [jnp.zeros((LANES, TQ), F32)] * n
    state, acc = _pipelined_blocks(i, heads, qk, vpu, pv, state0, acc0)
    outs = [acc[h] * (1.0 / state[h][1]) for h in heads]
    for g in range(n // 2):
        o_ref[:, g * LANES:(g + 1) * LANES] = jnp.where(sub_lo, outs[2 * g], outs[2 * g + 1]).T.astype(BF16)


def _fox(qkv, vt, c, ct, batch, seq):
    nq = seq // TQ
    return pl.pallas_call(
        _fox_kernel,
        grid=(batch, nq),
        in_specs=[
            pl.BlockSpec((TQ, W_FOX), lambda b, i: (b * nq + i, 0)),
            pl.BlockSpec((seq, W_FOX), lambda b, i: (b, 1)),
            pl.BlockSpec((None, W_FOX, seq), lambda b, i: (b, 0, 0)),
            pl.BlockSpec((seq, LANES), lambda b, i: (b, 0)),
            pl.BlockSpec((None, 8, seq), lambda b, i: (b, 0, 0)),
        ],
        out_specs=pl.BlockSpec((TQ, W_FOX), lambda b, i: (b * nq + i, 0)),
        out_shape=jax.ShapeDtypeStruct((batch * seq, W_FOX), BF16),
        compiler_params=_cparams(("arbitrary", "arbitrary")),
        name="fox_attn",
    )(qkv, qkv, vt, c, ct)


def _sb_kernel(q_ref, k_ref, vt_ref, o_ref):
    i = pl.program_id(1)
    halves, sub_lo = _pair_masks()
    row = lax.broadcasted_iota(I32, (TQ, TQ), 0)
    col = lax.broadcasted_iota(I32, (TQ, TQ), 1)
    strict_t = row < col
    half_blk = TQ // 2
    tri = (col >= row)[:half_blk, :half_blk].astype(BF16)
    tri2 = jnp.concatenate([tri, tri], axis=1)
    heads = tuple(range(N_HEADS_SB))
    pair_cols = lambda h: slice((h // 2) * LANES, (h // 2 + 1) * LANES)
    qm = []
    for h in heads:
        q = q_ref[:, pair_cols(h)]
        qm.append(jnp.where(halves[h % 2], q, jnp.zeros_like(q)))

    def qk(h, blk):
        return _dot_nt(k_ref[_rows(blk), pair_cols(h)], qm[h])

    def vpu(h, blk, z, tail, masked):
        nz = -z
        l1m = jnp.minimum(nz, 0.0) - jnp.log(1.0 + jnp.exp(jnp.minimum(z, nz)))
        if masked:
            l1m = jnp.where(strict_t, l1m, 0.0)
        hi = l1m.astype(BF16)
        lo = (l1m - hi.astype(F32)).astype(BF16)
        sum_a = jnp.sum(l1m[:half_blk], axis=0, keepdims=True)
        sum_b = jnp.sum(l1m[half_blk:], axis=0, keepdims=True)
        run_a = _dot(tri2, jnp.concatenate([hi[:half_blk], lo[:half_blk]], axis=0)) + sum_b
        run_b = _dot(tri2, jnp.concatenate([hi[half_blk:], lo[half_blk:]], axis=0))
        a = jnp.exp(z + jnp.concatenate([run_a, run_b], axis=0) + tail)
        if masked:
            a = jnp.where(strict_t, a, 0.0)
        return tail + (sum_a + sum_b), a.astype(BF16)

    def pv(h, blk, a, acc):
        return acc + _dot(vt_ref[pair_cols(h), _rows(blk)], a)

    n = len(heads)
    _, acc = _pipelined_blocks(i, heads, qk, vpu, pv, [jnp.zeros((1, TQ), F32)] * n,
                               [jnp.zeros((LANES, TQ), F32)] * n)
    acc = list(acc) + [jnp.zeros((LANES, TQ), F32)]
    for g in range(W_PAIRS // PAIR_W):
        o_ref[:, g * LANES:(g + 1) * LANES] = jnp.where(sub_lo, acc[2 * g], acc[2 * g + 1]).T.astype(BF16)


def _sb(qkv, vt, batch, seq):
    nq = seq // TQ
    return pl.pallas_call(
        _sb_kernel,
        grid=(batch, nq),
        in_specs=[
            pl.BlockSpec((TQ, W_PAIRS), lambda b, i: (b * nq + i, 0)),
            pl.BlockSpec((seq, W_PAIRS), lambda b, i: (b, 1)),
            pl.BlockSpec((None, W_PAIRS, seq), lambda b, i: (b, 0, 0)),
        ],
        out_specs=pl.BlockSpec((TQ, W_PAIRS), lambda b, i: (b * nq + i, 0)),
        out_shape=jax.ShapeDtypeStruct((batch * seq, W_PAIRS), BF16),
        compiler_params=_cparams(("arbitrary", "arbitrary")),
        name="sb_attn",
    )(qkv, qkv, vt)


def _dsa_kernel(q_ref, kk_ref, vt_ref, wt_ref, y_ref, key_ref, *, n_kb):
    i = 2 * (n_kb - 1) + pl.program_id(1)
    k_sel = jnp.minimum(TOPK_MAX, QB_DSA * (i + 1))
    lane = lax.broadcasted_iota(I32, (1, LANES), 1)
    lo_half = lane < HEAD_DIM

    qb = q_ref[...].astype(F32)
    tiles = [qb[:, k * LANES:(k + 1) * LANES] for k in range(GC_ROPE_W // LANES)]
    swap = lambda t: pltpu.roll(t, HEAD_DIM, 1)
    keep_lo = lambda t: jnp.where(lo_half, t, 0.0)
    keep_hi = lambda t: jnp.where(lo_half, 0.0, t)
    r_att = jnp.concatenate(
        [keep_lo(tiles[0]), keep_lo(swap(tiles[0])), keep_lo(tiles[1]), keep_lo(swap(tiles[1])),
         keep_lo(tiles[2])], axis=0).astype(BF16)
    r_idx = jnp.concatenate(
        [keep_hi(tiles[2]), keep_hi(swap(tiles[3])), keep_hi(tiles[3]), keep_hi(swap(tiles[4]))],
        axis=0).astype(BF16)
    wt = wt_ref[...]
    q_chunk = (i * QB_DSA + lane) // CHUNK
    k_chunk0 = lax.broadcasted_iota(I32, (KB_DSA, 1), 0) // CHUNK

    def kslice(jb):
        return slice(jb * KB_DSA, (jb + 1) * KB_DSA)

    def score_block(jb):
        raw = _dot_nt(kk_ref[kslice(jb), :], r_idx)
        sc = jnp.zeros((KB_DSA, LANES), F32)
        for h in range(N_IDX_HEADS):
            sc = sc + jnp.maximum(raw[:, h * LANES:(h + 1) * LANES], 0.0) * wt[h:h + 1, :]
        adm = (k_chunk0 + jb * (KB_DSA // CHUNK)) <= q_chunk
        sc = jnp.where(adm, sc, -jnp.inf)
        bits = lax.bitcast_convert_type(sc, I32)
        key = jnp.where(bits < 0, bits ^ 0x7FFFFFFF, bits)
        key_ref[kslice(jb), :] = jnp.where(sc == 0.0, 0, key)

    for jb in range(n_kb):
        score_block(jb)

    def count_ge(cand):
        acc = jnp.zeros((8, LANES), I32)
        for jb in range(n_kb):
            ge = (key_ref[kslice(jb), :] >= cand).astype(I32)
            acc = acc + jnp.sum(ge.reshape(KB_DSA // 8, 8, LANES), axis=0)
        return jnp.sum(acc, axis=0, keepdims=True)

    def bisect(it, lo):
        cand = lo + jnp.left_shift(jnp.int32(1), 31 - it)
        return jnp.where(count_ge(cand) >= k_sel, cand, lo)

    thr = lax.fori_loop(0, 32, bisect, jnp.full((1, LANES), INT_MIN, I32))
    need = (k_sel - count_ge(thr + 1)).astype(F32)
    neg_inf_key = jnp.int32(-8388608) ^ 0x7FFFFFFF
    before = (lax.broadcasted_iota(I32, (KB_DSA, KB_DSA), 1)
              < lax.broadcasted_iota(I32, (KB_DSA, KB_DSA), 0)).astype(BF16)

    def scores(jb, ties):
        keyb = key_ref[kslice(jb), :]
        eq = keyb == thr
        eqf = jnp.where(eq, 1.0, 0.0)
        rank = _dot(before, eqf.astype(BF16)) + ties
        sel = ((keyb > thr) | (eq & (rank < need))) & (keyb > neg_inf_key)
        bias = jnp.where(sel, 0.0, NEG_BIG)
        st = _dot_nt(kk_ref[kslice(jb), :], r_att)
        st = st + jnp.concatenate([bias] * N_HEADS_DSA, axis=1)
        return st, ties + jnp.sum(eqf, axis=0, keepdims=True)

    def softmax_step(st, m, l):
        m_new = jnp.maximum(m, jnp.max(st, axis=0, keepdims=True))
        alpha = jnp.exp(m - m_new)
        p = jnp.exp(st - m_new)
        return m_new, alpha * l + jnp.sum(p, axis=0, keepdims=True), (alpha, p.astype(BF16))

    def values(jb, pend, acc):
        alpha, p = pend
        return alpha * acc + _dot(vt_ref[:, kslice(jb)], p)

    wide = N_HEADS_DSA * LANES

    ties = jnp.zeros((1, LANES), F32)
    m = jnp.full((1, wide), NEG_BIG, F32)
    l = jnp.zeros((1, wide), F32)
    acc = jnp.zeros((HEAD_DIM, wide), F32)
    for jb in range(n_kb):
        st, ties = scores(jb, ties)
        m, l, pend = softmax_step(st, m, l)
        acc = values(jb, pend, acc)
    o = acc * (1.0 / l)
    heads = [o[:, h * LANES:(h + 1) * LANES] for h in range(N_HEADS_DSA)]
    heads.append(jnp.zeros_like(heads[0]))
    for g in range(W_PAIRS // PAIR_W):
        y_ref[:, g * LANES:(g + 1) * LANES] = jnp.concatenate(
            [heads[2 * g], heads[2 * g + 1]], axis=0).T.astype(BF16)


def _dsa(dq, dkk, vt, wt, batch, seq):
    nq = seq // QB_DSA
    per_call = KB_DSA // QB_DSA
    outs = []
    for n_kb in range(1, seq // KB_DSA + 1):
        first = per_call * (n_kb - 1)
        outs.append(pl.pallas_call(
            functools.partial(_dsa_kernel, n_kb=n_kb),
            grid=(batch, per_call),
            in_specs=[
                pl.BlockSpec((QB_DSA, GC_ROPE_W), lambda b, j, first=first: (b * nq + first + j, 0)),
                pl.BlockSpec((seq, LANES), lambda b, j: (b, 0)),
                pl.BlockSpec((None, HEAD_DIM, seq), lambda b, j: (b, 0, 0)),
                pl.BlockSpec((None, 8, QB_DSA), lambda b, j, first=first: (b, 0, first + j)),
            ],
            out_specs=pl.BlockSpec((QB_DSA, W_PAIRS), lambda b, j: (b * per_call + j, 0)),
            out_shape=jax.ShapeDtypeStruct((batch * KB_DSA, W_PAIRS), BF16),
            scratch_shapes=[pltpu.VMEM((n_kb * KB_DSA, LANES), I32)],
            compiler_params=_cparams(("arbitrary", "arbitrary")),
            name=f"dsa_attn_{n_kb}",
        )(dq, dkk, vt, wt))
    y = jnp.stack([o.reshape(batch, KB_DSA, W_PAIRS) for o in outs], axis=1)
    return y.reshape(batch * seq, W_PAIRS)


def _merge_kernel(x_ref, ya_ref, yb_ref, yc_ref, gt_ref, wa_ref, wb_ref, wc_ref, wo_ref, o_ref):
    merged = None
    for k, (y_ref, w_ref) in enumerate(((ya_ref, wa_ref), (yb_ref, wb_ref), (yc_ref, wc_ref))):
        gate = gt_ref[:, k * D_MODEL:(k + 1) * D_MODEL].astype(F32)
        term = gate * _dot(y_ref[...], w_ref[...])
        merged = term if merged is None else merged + term
    o_ref[...] = x_ref[...] + _dot(merged.astype(BF16), wo_ref[...])


def _merge(x, ya, yb, yc, gates, wa, wb, wc, wo):
    m = x.shape[0]
    row = lambda width: pl.BlockSpec((TM_MERGE, width), lambda i: (i, 0))
    return pl.pallas_call(
        _merge_kernel,
        grid=(m // TM_MERGE,),
        in_specs=[row(D_MODEL), row(W_PAIRS), row(W_PAIRS), row(W_PAIRS), row(GG_W),
                  _resident(wa.shape), _resident(wb.shape), _resident(wc.shape), _resident(wo.shape)],
        out_specs=row(D_MODEL),
        out_shape=jax.ShapeDtypeStruct((m, D_MODEL), F32),
        compiler_params=_cparams(("arbitrary",)),
        name="merge_out",
    )(x, ya, yb, yc, gates, wa, wb, wc, wo)


def _pad_heads(w, axis):
    pad = [(0, 0)] * w.ndim
    pad[axis] = (0, W_PAIRS - w.shape[axis])
    return jnp.pad(w, pad)


def _layout_w_in(w_in):
    sizes = [W_FOX, W_FOX, W_FOX, N_HEADS_FOX, W_SB, W_SB, W_SB, W_DSA, KV_LATENT,
             N_IDX_HEADS * IDX_DIM, IDX_DIM, N_IDX_HEADS, 3 * D_MODEL]
    pieces, acc = [], 0
    for s in sizes:
        pieces.append(w_in[:, acc:acc + s])
        acc += s
    q_a, k_a, v_a, f_a, q_b, k_b, v_b, q_c, c_kv, q_i, k_i, w_i, gates = pieces
    scale = HEAD_DIM ** -0.5
    zeros = lambda n: jnp.zeros((D_MODEL, n), w_in.dtype)
    cols = [q_a * scale, k_a, v_a,
            _pad_heads(q_b * scale, 1), _pad_heads(k_b, 1), _pad_heads(v_b, 1),
            q_c * scale, q_i, k_i, c_kv,
            f_a, zeros(8 - N_HEADS_FOX), w_i, zeros(GS_W - 8 - N_IDX_HEADS),
            gates]
    out = jnp.concatenate(cols, axis=1).astype(BF16)
    assert out.shape == (D_MODEL, D_IN_PAD)
    return out


def _layout_ffn(w_gu, w_down):
    wg = w_gu[:, :D_FF].reshape(D_MODEL, N_FF_CHUNKS, FF_CHUNK)
    wu = w_gu[:, D_FF:].reshape(D_MODEL, N_FF_CHUNKS, FF_CHUNK)
    w1 = jnp.concatenate([wg, wu], axis=2).transpose(1, 0, 2).astype(BF16)
    w2 = w_down.reshape(N_FF_CHUNKS, FF_CHUNK, D_MODEL).astype(BF16)
    return w1, w2


def _rope_tables(seq):
    half = ROPE_DIM // 2
    pos = jnp.arange(seq, dtype=F32)
    inv = ROPE_THETA ** (-jnp.arange(0, ROPE_DIM, 2, dtype=F32) / ROPE_DIM)
    ang = pos[:, None] * inv[None, :]
    cos, sin = jnp.cos(ang), jnp.sin(ang)
    ones = jnp.ones((seq, HEAD_DIM - ROPE_DIM), F32)
    zeros_rest = jnp.zeros((seq, HEAD_DIM - ROPE_DIM), F32)
    zeros_half = jnp.zeros((seq, half), F32)
    c = jnp.concatenate([cos, cos, ones], axis=1)
    s1 = jnp.concatenate([-sin, zeros_half, zeros_rest], axis=1)
    s2 = jnp.concatenate([zeros_half, sin, zeros_rest], axis=1)
    two = lambda t: jnp.concatenate([t, t], axis=1)
    return two(c), two(s1), two(s2)


@jax.jit
def kernel(x, g_ffn1, w_ffn1_gu, w_ffn1_down, g_mix, w_in, b_forget, g_kv_latent, w_kv_up, g_idx_k,
           w_up_fox, w_up_sb, w_up_dsa, w_out, g_ffn2, w_ffn2_gu, w_ffn2_down, g_final):
    batch, seq, _ = x.shape
    depth = g_ffn1.shape[0]
    rc, rs1, rs2 = _rope_tables(seq)
    xf = x.reshape(batch * seq, D_MODEL)
    row = lambda v: v.reshape(1, -1).astype(F32)
    gf = row(g_final)
    for layer in range(depth):
        w1, w2 = _layout_ffn(w_ffn1_gu[layer], w_ffn1_down[layer])
        xf = _ffn(xf, row(g_ffn1[layer]), w1, w2, gf, False)

        gki = jnp.concatenate([jnp.ones((IDX_DIM,), F32), g_idx_k[layer]]).reshape(1, LANES)
        qk_a, qk_b, vt_a, vt_b, dq, dkk, vt, small, gates = _proj(
            xf, row(g_mix[layer]), _layout_w_in(w_in[layer]), w_kv_up[layer].astype(BF16),
            row(g_kv_latent[layer]), gki, rc, rs1, rs2, batch, seq)
        bcol = jnp.zeros((LANES, 1), F32).at[:N_HEADS_FOX, 0].set(b_forget[layer])
        c, ct, wt = _gate(small, bcol, batch, seq)
        y_a = _fox(qk_a, vt_a, c, ct, batch, seq)
        y_b = _sb(qk_b, vt_b, batch, seq)
        y_c = _dsa(dq, dkk, vt, wt, batch, seq)
        xf = _merge(xf, y_a, y_b, y_c, gates,
                    w_up_fox[layer].astype(BF16), _pad_heads(w_up_sb[layer], 0).astype(BF16),
                    _pad_heads(w_up_dsa[layer], 0).astype(BF16), w_out[layer].astype(BF16))

        w1, w2 = _layout_ffn(w_ffn2_gu[layer], w_ffn2_down[layer])
        xf = _ffn(xf, row(g_ffn2[layer]), w1, w2, gf, layer == depth - 1)
    return xf.reshape(batch, seq, D_MODEL)
```

```python
import functools

import jax
import jax.numpy as jnp
from jax import lax
from jax.experimental import pallas as pl
from jax.experimental.pallas import tpu as pltpu

F32 = jnp.float32
BF16 = jnp.bfloat16
I32 = jnp.int32

D_MODEL = 1024
HEAD_DIM = 64
N_HEADS_FOX = 6
N_HEADS_SB = 5
N_HEADS_DSA = 5
W_FOX = N_HEADS_FOX * HEAD_DIM
W_SB = N_HEADS_SB * HEAD_DIM
W_DSA = N_HEADS_DSA * HEAD_DIM
KV_LATENT = 128
N_IDX_HEADS = 4
IDX_DIM = 64
CHUNK = 64
TOPK_MAX = 256
ROPE_THETA = 500000.0
ROPE_DIM = HEAD_DIM // 4
D_FF = 2816
EPS = 1e-6

LANES = 128
PAIR_W = 2 * HEAD_DIM
W_PAIRS = 3 * PAIR_W
VMEM_LIMIT = 56 * 1024 * 1024

GA_W = 3 * W_FOX
GB_W = 3 * W_PAIRS
GC_ROPE_W = W_DSA + N_IDX_HEADS * IDX_DIM + IDX_DIM
GC_W = GC_ROPE_W + KV_LATENT
GS_W = LANES
GG_W = 3 * D_MODEL
GA_0 = 0
GB_0 = GA_0 + GA_W
GC_0 = GB_0 + GB_W
GS_0 = GC_0 + GC_W
GG_0 = GS_0 + GS_W
D_IN_PAD = GG_0 + GG_W

FF_CHUNK = 256
N_FF_CHUNKS = D_FF // FF_CHUNK

TM_FFN = 512
TM_PROJ = 512
TQ = 256
QB_DSA = 128
KB_DSA = 256
NEG_BIG = -1e30
INT_MIN = -2147483648


def _cparams(sem):
    return pltpu.CompilerParams(dimension_semantics=sem, vmem_limit_bytes=VMEM_LIMIT)


def _resident(shape):
    nd = len(shape)
    return pl.BlockSpec(shape, lambda *_: (0,) * nd, pipeline_mode=pl.Buffered(1))


def _rms(x, g):
    ms = jnp.mean(x * x, axis=-1, keepdims=True)
    return x * lax.rsqrt(ms + EPS) * g


def _dot(a, b):
    return jnp.dot(a, b, preferred_element_type=F32)


def _dot_nt(a, b):
    return lax.dot_general(a, b, (((1,), (1,)), ((), ())), preferred_element_type=F32)


def _log_sigmoid(z):
    return jnp.minimum(z, 0.0) - jnp.log1p(jnp.exp(-jnp.abs(z)))


def _ffn_kernel(x_ref, g_ref, w1_ref, w2_ref, gf_ref, o_ref, acc_ref, *, final_norm):
    o_ref[...] = _ffn_rows(x_ref[...], g_ref, w1_ref, w2_ref, gf_ref, acc_ref, final_norm)


def _ffn_rows(x, g_ref, w1_ref, w2_ref, gf_ref, acc_ref, final_norm):
    h = _rms(x, g_ref[...]).astype(BF16)
    for j in range(N_FF_CHUNKS):
        gu = _dot(h, w1_ref[j])
        g = gu[:, :FF_CHUNK]
        u = gu[:, FF_CHUNK:]
        a = (g * jax.nn.sigmoid(g) * u).astype(BF16)
        d = _dot(a, w2_ref[j])
        if j == 0:
            acc_ref[...] = d
        else:
            acc_ref[...] += d
    y = x + 0.5 * acc_ref[...]
    if final_norm:
        y = _rms(y, gf_ref[...])
    return y


def _ffn(x, g, w1, w2, gf, final_norm):
    m = x.shape[0]
    return pl.pallas_call(
        functools.partial(_ffn_kernel, final_norm=final_norm),
        grid=(m // TM_FFN,),
        in_specs=[
            pl.BlockSpec((TM_FFN, D_MODEL), lambda i: (i, 0)),
            _resident((1, D_MODEL)),
            _resident(w1.shape),
            _resident(w2.shape),
            _resident((1, D_MODEL)),
        ],
        out_specs=pl.BlockSpec((TM_FFN, D_MODEL), lambda i: (i, 0)),
        out_shape=jax.ShapeDtypeStruct((m, D_MODEL), F32),
        scratch_shapes=[pltpu.VMEM((TM_FFN, D_MODEL), F32)],
        compiler_params=_cparams(("arbitrary",)),
        name="ffn",
    )(x, g, w1, w2, gf)


def _rope_tile(t, c, s1, s2):
    return t * c + pltpu.roll(t, LANES - ROPE_DIM // 2, 1) * s1 + pltpu.roll(t, ROPE_DIM // 2, 1) * s2


def _proj_kernel(x_ref, g_ref, w_ref, wkv_ref, gkv_ref, gki_ref, rc_ref, rs1_ref, rs2_ref,
                 qa_ref, qb_ref, vat_ref, vbt_ref, dq_ref, dkk_ref, vt_ref, sm_ref, gt_ref):
    h = _rms(x_ref[...], g_ref[...]).astype(BF16)

    def proj(c0, width):
        return _dot(h, w_ref[:, c0:c0 + width])

    for g0, qk_ref, v_ref in ((GA_0, qa_ref, vat_ref), (GB_0, qb_ref, vbt_ref)):
        for c0 in range(0, 2 * W_PAIRS, W_PAIRS):
            qk_ref[:, c0:c0 + W_PAIRS] = proj(g0 + c0, W_PAIRS).astype(BF16)
        v = proj(g0 + 2 * W_PAIRS, W_PAIRS)
        for k in range(W_PAIRS // LANES):
            v_ref[k * LANES:(k + 1) * LANES, :] = v[:, k * LANES:(k + 1) * LANES].T.astype(BF16)
    sm_ref[...] = proj(GS_0, GS_W)
    for c0 in range(0, GG_W, 512):
        gt_ref[:, c0:c0 + 512] = jax.nn.sigmoid(proj(GG_0 + c0, 512)).astype(BF16)

    r = proj(GC_0, GC_W)
    lane = lax.broadcasted_iota(I32, (1, LANES), 1)
    hi_half = lane >= HEAD_DIM
    rc, rs1, rs2 = rc_ref[...], rs1_ref[...], rs2_ref[...]
    n_tiles = GC_ROPE_W // LANES
    last = None
    for k in range(n_tiles):
        t = r[:, k * LANES:(k + 1) * LANES]
        if k == n_tiles - 1:
            ms = jnp.sum(jnp.where(hi_half, t * t, 0.0), axis=-1, keepdims=True) * (1.0 / IDX_DIM)
            t = t * jnp.where(hi_half, lax.rsqrt(ms + EPS) * gki_ref[...], 1.0)
        t = _rope_tile(t, rc, rs1, rs2)
        dq_ref[:, k * LANES:(k + 1) * LANES] = t.astype(BF16)
        last = t
    ckv = _rms(r[:, GC_ROPE_W:GC_W], gkv_ref[...]).astype(BF16)
    kv = _dot(ckv, wkv_ref[...])
    lo_half = jnp.logical_not(hi_half)
    kv = _rope_tile(kv, jnp.where(lo_half, rc, 1.0), jnp.where(lo_half, rs1, 0.0),
                    jnp.where(lo_half, rs2, 0.0))
    dkk_ref[...] = jnp.where(lo_half, kv, last).astype(BF16)
    vt_ref[...] = kv.T[HEAD_DIM:, :].astype(BF16)


def _proj(x, g, w, wkv, gkv, gki, rc, rs1, rs2, batch, seq):
    m = x.shape[0]
    spb = seq // TM_PROJ
    row = lambda width: pl.BlockSpec((TM_PROJ, width), lambda i: (i, 0))
    pos = pl.BlockSpec((TM_PROJ, LANES), lambda i: (i % spb, 0))
    return pl.pallas_call(
        _proj_kernel,
        grid=(m // TM_PROJ,),
        in_specs=[
            row(D_MODEL), _resident((1, D_MODEL)), _resident(w.shape), _resident(wkv.shape),
            _resident((1, KV_LATENT)), _resident((1, LANES)), pos, pos, pos,
        ],
        out_specs=[
            row(2 * W_PAIRS), row(2 * W_PAIRS),
            pl.BlockSpec((None, W_PAIRS, TM_PROJ), lambda i: (i // spb, 0, i % spb)),
            pl.BlockSpec((None, W_PAIRS, TM_PROJ), lambda i: (i // spb, 0, i % spb)),
            row(GC_ROPE_W), row(LANES),
            pl.BlockSpec((None, HEAD_DIM, TM_PROJ), lambda i: (i // spb, 0, i % spb)),
            row(GS_W), row(GG_W),
        ],
        out_shape=[
            jax.ShapeDtypeStruct((m, 2 * W_PAIRS), BF16),
            jax.ShapeDtypeStruct((m, 2 * W_PAIRS), BF16),
            jax.ShapeDtypeStruct((batch, W_PAIRS, seq), BF16),
            jax.ShapeDtypeStruct((batch, W_PAIRS, seq), BF16),
            jax.ShapeDtypeStruct((m, GC_ROPE_W), BF16),
            jax.ShapeDtypeStruct((m, LANES), BF16),
            jax.ShapeDtypeStruct((batch, HEAD_DIM, seq), BF16),
            jax.ShapeDtypeStruct((m, GS_W), F32),
            jax.ShapeDtypeStruct((m, GG_W), BF16),
        ],
        compiler_params=_cparams(("arbitrary",)),
        name="in_proj",
    )(x, g, w, wkv, gkv, gki, rc, rs1, rs2)


def _gate_kernel(s_ref, b_ref, c_ref, ct_ref, wt_ref):
    xt = s_ref[...].T
    c = _log_sigmoid(xt + b_ref[...])
    seq = c.shape[1]
    lane = lax.broadcasted_iota(I32, c.shape, 1)
    d = 1
    while d < seq:
        c = c + jnp.where(lane >= d, pltpu.roll(c, d, 1), 0.0)
        d *= 2
    ct_ref[...] = c[0:8]
    wt_ref[...] = xt[8:16]
    c_ref[...] = c.T


def _gate(small, bcol, batch, seq):
    return pl.pallas_call(
        _gate_kernel,
        grid=(batch,),
        in_specs=[pl.BlockSpec((seq, LANES), lambda b: (b, 0)), _resident((LANES, 1))],
        out_specs=[
            pl.BlockSpec((seq, LANES), lambda b: (b, 0)),
            pl.BlockSpec((None, 8, seq), lambda b: (b, 0, 0)),
            pl.BlockSpec((None, 8, seq), lambda b: (b, 0, 0)),
        ],
        out_shape=[
            jax.ShapeDtypeStruct((batch * seq, LANES), F32),
            jax.ShapeDtypeStruct((batch, 8, seq), F32),
            jax.ShapeDtypeStruct((batch, 8, seq), F32),
        ],
        compiler_params=_cparams(("arbitrary",)),
        name="gate_cumsum",
    )(small, bcol)


def _rows(blk):
    return pl.ds(pl.multiple_of(blk * TQ, TQ), TQ)


def _pipelined_blocks(i, heads, qk, vpu, pv, state0, acc0):
    n = len(heads)
    s_diag = [qk(h, i) for h in heads]
    s_next = tuple(qk(h, jnp.maximum(i - 1, 0)) for h in heads)
    first = [vpu(heads[k], i, s_diag[k], state0[k], True) for k in range(n)]
    state = tuple(f[0] for f in first)
    pend = tuple(f[1] for f in first)

    def step(t, carry):
        s_cur, state, pend, acc = carry
        blk = i - t
        s_nxt = tuple(qk(h, jnp.maximum(blk - 1, 0)) for h in heads)
        acc = tuple(pv(heads[k], blk + 1, pend[k], acc[k]) for k in range(n))
        new = [vpu(heads[k], blk, s_cur[k], state[k], False) for k in range(n)]
        return s_nxt, tuple(x[0] for x in new), tuple(x[1] for x in new), acc

    carry = lax.fori_loop(0, i // 2, lambda u, c: step(2 * u + 2, step(2 * u + 1, c)),
                          (s_next, state, pend, tuple(acc0)))
    _, state, pend, acc = lax.cond(i % 2 == 1, lambda c: step(i, c), lambda c: c, carry)
    acc = tuple(pv(heads[k], 0, pend[k], acc[k]) for k in range(n))
    return state, acc


def _pair_masks():
    lane = lax.broadcasted_iota(I32, (1, LANES), 1)
    sub = lax.broadcasted_iota(I32, (LANES, 1), 0)
    return (lane < HEAD_DIM, lane >= HEAD_DIM), sub < HEAD_DIM


def _fox_kernel(q_ref, k_ref, vt_ref, c_ref, ct_ref, o_ref):
    i = pl.program_id(1)
    halves, sub_lo = _pair_masks()
    causal_t = (lax.broadcasted_iota(I32, (TQ, TQ), 0) <= lax.broadcasted_iota(I32, (TQ, TQ), 1))
    heads = tuple(range(N_HEADS_FOX))
    pair_cols = lambda h: slice((h // 2) * LANES, (h // 2 + 1) * LANES)
    qm = []
    for h in heads:
        q = q_ref[:, pair_cols(h)]
        qm.append(jnp.where(halves[h % 2], q, jnp.zeros_like(q)))
    ct_q = ct_ref[:, _rows(i)]

    def qk(h, blk):
        return _dot_nt(k_ref[_rows(blk), pair_cols(h)], qm[h])

    def vpu(h, blk, s, state, masked):
        m, l = state
        cq = ct_q[h:h + 1, :]
        u = s - c_ref[_rows(blk), h:h + 1]
        if masked:
            u = jnp.where(causal_t, u, -jnp.inf)
        m_new = jnp.maximum(m, jnp.max(u, axis=0, keepdims=True) + cq)
        alpha = jnp.exp(m - m_new)
        p = jnp.exp(u + (cq - m_new))
        l = alpha * l + jnp.sum(p, axis=0, keepdims=True)
        return (m_new, l), (alpha, p.astype(BF16))

    def pv(h, blk, pend, acc):
        alpha, p = pend
        return alpha * acc + _dot(vt_ref[pair_cols(h), _rows(blk)], p)

    n = len(heads)
    state0 = [(jnp.full((1, TQ), -jnp.inf, F32), jnp.zeros((1, TQ), F32))] * n
    acc0 = [jnp.zeros((LANES, TQ), F32)] * n
    state, acc = _pipelined_blocks(i, heads, qk, vpu, pv, state0, acc0)
    outs = [acc[h] * (1.0 / state[h][1]) for h in heads]
    for g in range(n // 2):
        o_ref[:, g * LANES:(g + 1) * LANES] = jnp.where(sub_lo, outs[2 * g], outs[2 * g + 1]).T.astype(BF16)


def _fox(qkv, vt, c, ct, batch, seq):
    nq = seq // TQ
    return pl.pallas_call(
        _fox_kernel,
        grid=(batch, nq),
        in_specs=[
            pl.BlockSpec((TQ, W_FOX), lambda b, i: (b * nq + i, 0)),
            pl.BlockSpec((seq, W_FOX), lambda b, i: (b, 1)),
            pl.BlockSpec((None, W_FOX, seq), lambda b, i: (b, 0, 0)),
            pl.BlockSpec((seq, LANES), lambda b, i: (b, 0)),
            pl.BlockSpec((None, 8, seq), lambda b, i: (b, 0, 0)),
        ],
        out_specs=pl.BlockSpec((TQ, W_FOX), lambda b, i: (b * nq + i, 0)),
        out_shape=jax.ShapeDtypeStruct((batch * seq, W_FOX), BF16),
        compiler_params=_cparams(("arbitrary", "arbitrary")),
        name="fox_attn",
    )(qkv, qkv, vt, c, ct)


def _sb_kernel(q_ref, k_ref, vt_ref, o_ref):
    i = pl.program_id(1)
    halves, sub_lo = _pair_masks()
    row = lax.broadcasted_iota(I32, (TQ, TQ), 0)
    col = lax.broadcasted_iota(I32, (TQ, TQ), 1)
    strict_t = row < col
    half_blk = TQ // 2
    tri = (col >= row)[:half_blk, :half_blk].astype(BF16)
    tri2 = jnp.concatenate([tri, tri], axis=1)
    heads = tuple(range(N_HEADS_SB))
    pair_cols = lambda h: slice((h // 2) * LANES, (h // 2 + 1) * LANES)
    qm = []
    for h in heads:
        q = q_ref[:, pair_cols(h)]
        qm.append(jnp.where(halves[h % 2], q, jnp.zeros_like(q)))

    def qk(h, blk):
        return _dot_nt(k_ref[_rows(blk), pair_cols(h)], qm[h])

    def vpu(h, blk, z, tail, masked):
        nz = -z
        l1m = jnp.minimum(nz, 0.0) - jnp.log(1.0 + jnp.exp(jnp.minimum(z, nz)))
        if masked:
            l1m = jnp.where(strict_t, l1m, 0.0)
        hi = l1m.astype(BF16)
        lo = (l1m - hi.astype(F32)).astype(BF16)
        sum_a = jnp.sum(l1m[:half_blk], axis=0, keepdims=True)
        sum_b = jnp.sum(l1m[half_blk:], axis=0, keepdims=True)
        run_a = _dot(tri2, jnp.concatenate([hi[:half_blk], lo[:half_blk]], axis=0)) + sum_b
        run_b = _dot(tri2, jnp.concatenate([hi[half_blk:], lo[half_blk:]], axis=0))
        a = jnp.exp(z + jnp.concatenate([run_a, run_b], axis=0) + tail)
        if masked:
            a = jnp.where(strict_t, a, 0.0)
        return tail + (sum_a + sum_b), a.astype(BF16)

    def pv(h, blk, a, acc):
        return acc + _dot(vt_ref[pair_cols(h), _rows(blk)], a)

    n = len(heads)
    _, acc = _pipelined_blocks(i, heads, qk, vpu, pv, [jnp.zeros((1, TQ), F32)] * n,
                               [jnp.zeros((LANES, TQ), F32)] * n)
    acc = list(acc) + [jnp.zeros((LANES, TQ), F32)]
    for g in range(W_PAIRS // PAIR_W):
        o_ref[:, g * LANES:(g + 1) * LANES] = jnp.where(sub_lo, acc[2 * g], acc[2 * g + 1]).T.astype(BF16)


def _sb(qkv, vt, batch, seq):
    nq = seq // TQ
    return pl.pallas_call(
        _sb_kernel,
        grid=(batch, nq),
        in_specs=[
            pl.BlockSpec((TQ, W_PAIRS), lambda b, i: (b * nq + i, 0)),
            pl.BlockSpec((seq, W_PAIRS), lambda b, i: (b, 1)),
            pl.BlockSpec((None, W_PAIRS, seq), lambda b, i: (b, 0, 0)),
        ],
        out_specs=pl.BlockSpec((TQ, W_PAIRS), lambda b, i: (b * nq + i, 0)),
        out_shape=jax.ShapeDtypeStruct((batch * seq, W_PAIRS), BF16),
        compiler_params=_cparams(("arbitrary", "arbitrary")),
        name="sb_attn",
    )(qkv, qkv, vt)


def _dsa_kernel(q_ref, kk_ref, vt_ref, wt_ref, y_ref, key_ref, *, n_kb):
    i = 2 * (n_kb - 1) + pl.program_id(1)
    k_sel = jnp.minimum(TOPK_MAX, QB_DSA * (i + 1))
    lane = lax.broadcasted_iota(I32, (1, LANES), 1)
    lo_half = lane < HEAD_DIM

    qb = q_ref[...].astype(F32)
    tiles = [qb[:, k * LANES:(k + 1) * LANES] for k in range(GC_ROPE_W // LANES)]
    swap = lambda t: pltpu.roll(t, HEAD_DIM, 1)
    keep_lo = lambda t: jnp.where(lo_half, t, 0.0)
    keep_hi = lambda t: jnp.where(lo_half, 0.0, t)
    r_att = jnp.concatenate(
        [keep_lo(tiles[0]), keep_lo(swap(tiles[0])), keep_lo(tiles[1]), keep_lo(swap(tiles[1])),
         keep_lo(tiles[2])], axis=0).astype(BF16)
    r_idx = jnp.concatenate(
        [keep_hi(tiles[2]), keep_hi(swap(tiles[3])), keep_hi(tiles[3]), keep_hi(swap(tiles[4]))],
        axis=0).astype(BF16)
    wt = wt_ref[...]
    q_chunk = (i * QB_DSA + lane) // CHUNK
    k_chunk0 = lax.broadcasted_iota(I32, (KB_DSA, 1), 0) // CHUNK

    def kslice(jb):
        return slice(jb * KB_DSA, (jb + 1) * KB_DSA)

    def score_block(jb):
        raw = _dot_nt(kk_ref[kslice(jb), :], r_idx)
        sc = jnp.zeros((KB_DSA, LANES), F32)
        for h in range(N_IDX_HEADS):
            sc = sc + jnp.maximum(raw[:, h * LANES:(h + 1) * LANES], 0.0) * wt[h:h + 1, :]
        adm = (k_chunk0 + jb * (KB_DSA // CHUNK)) <= q_chunk
        sc = jnp.where(adm, sc, -jnp.inf)
        bits = lax.bitcast_convert_type(sc, I32)
        key = jnp.where(bits < 0, bits ^ 0x7FFFFFFF, bits)
        key_ref[kslice(jb), :] = jnp.where(sc == 0.0, 0, key)

    for jb in range(n_kb):
        score_block(jb)

    def count_ge(cand):
        acc = jnp.zeros((8, LANES), I32)
        for jb in range(n_kb):
            ge = (key_ref[kslice(jb), :] >= cand).astype(I32)
            acc = acc + jnp.sum(ge.reshape(KB_DSA // 8, 8, LANES), axis=0)
        return jnp.sum(acc, axis=0, keepdims=True)

    def bisect(it, lo):
        cand = lo + jnp.left_shift(jnp.int32(1), 31 - it)
        return jnp.where(count_ge(cand) >= k_sel, cand, lo)

    thr = lax.fori_loop(0, 32, bisect, jnp.full((1, LANES), INT_MIN, I32))
    need = (k_sel - count_ge(thr + 1)).astype(F32)
    neg_inf_key = jnp.int32(-8388608) ^ 0x7FFFFFFF
    before = (lax.broadcasted_iota(I32, (KB_DSA, KB_DSA), 1)
              < lax.broadcasted_iota(I32, (KB_DSA, KB_DSA), 0)).astype(BF16)

    def scores(jb, ties):
        keyb = key_ref[kslice(jb), :]
        eq = keyb == thr
        eqf = jnp.where(eq, 1.0, 0.0)
        rank = _dot(before, eqf.astype(BF16)) + ties
        sel = ((keyb > thr) | (eq & (rank < need))) & (keyb > neg_inf_key)
        bias = jnp.where(sel, 0.0, NEG_BIG)
        st = _dot_nt(kk_ref[kslice(jb), :], r_att)
        st = st + jnp.concatenate([bias] * N_HEADS_DSA, axis=1)
        return st, ties + jnp.sum(eqf, axis=0, keepdims=True)

    def softmax_step(st, m, l):
        m_new = jnp.maximum(m, jnp.max(st, axis=0, keepdims=True))
        alpha = jnp.exp(m - m_new)
        p = jnp.exp(st - m_new)
        return m_new, alpha * l + jnp.sum(p, axis=0, keepdims=True), (alpha, p.astype(BF16))

    def values(jb, pend, acc):
        alpha, p = pend
        return alpha * acc + _dot(vt_ref[:, kslice(jb)], p)

    wide = N_HEADS_DSA * LANES

    ties = jnp.zeros((1, LANES), F32)
    m = jnp.full((1, wide), NEG_BIG, F32)
    l = jnp.zeros((1, wide), F32)
    acc = jnp.zeros((HEAD_DIM, wide), F32)
    for jb in range(n_kb):
        st, ties = scores(jb, ties)
        m, l, pend = softmax_step(st, m, l)
        acc = values(jb, pend, acc)
    o = acc * (1.0 / l)
    heads = [o[:, h * LANES:(h + 1) * LANES] for h in range(N_HEADS_DSA)]
    heads.append(jnp.zeros_like(heads[0]))
    for g in range(W_PAIRS // PAIR_W):
        y_ref[:, g * LANES:(g + 1) * LANES] = jnp.concatenate(
            [heads[2 * g], heads[2 * g + 1]], axis=0).T.astype(BF16)


def _dsa(dq, dkk, vt, wt, batch, seq):
    nq = seq // QB_DSA
    per_call = KB_DSA // QB_DSA
    outs = []
    for n_kb in range(1, seq // KB_DSA + 1):
        first = per_call * (n_kb - 1)
        outs.append(pl.pallas_call(
            functools.partial(_dsa_kernel, n_kb=n_kb),
            grid=(batch, per_call),
            in_specs=[
                pl.BlockSpec((QB_DSA, GC_ROPE_W), lambda b, j, first=first: (b * nq + first + j, 0)),
                pl.BlockSpec((seq, LANES), lambda b, j: (b, 0)),
                pl.BlockSpec((None, HEAD_DIM, seq), lambda b, j: (b, 0, 0)),
                pl.BlockSpec((None, 8, QB_DSA), lambda b, j, first=first: (b, 0, first + j)),
            ],
            out_specs=pl.BlockSpec((QB_DSA, W_PAIRS), lambda b, j: (b * per_call + j, 0)),
            out_shape=jax.ShapeDtypeStruct((batch * KB_DSA, W_PAIRS), BF16),
            scratch_shapes=[pltpu.VMEM((n_kb * KB_DSA, LANES), I32)],
            compiler_params=_cparams(("arbitrary", "arbitrary")),
            name=f"dsa_attn_{n_kb}",
        )(dq, dkk, vt, wt))
    y = jnp.stack([o.reshape(batch, KB_DSA, W_PAIRS) for o in outs], axis=1)
    return y.reshape(batch * seq, W_PAIRS)


def _merge_ffn_kernel(x_ref, ya_ref, yb_ref, yc_ref, gt_ref, wa_ref, wb_ref, wc_ref, wo_ref,
                      g_ref, w1_ref, w2_ref, gf_ref, o_ref, acc_ref, *, final_norm):
    merged = None
    for k, (y_ref, w_ref) in enumerate(((ya_ref, wa_ref), (yb_ref, wb_ref), (yc_ref, wc_ref))):
        gate = gt_ref[:, k * D_MODEL:(k + 1) * D_MODEL].astype(F32)
        term = gate * _dot(y_ref[...], w_ref[...])
        merged = term if merged is None else merged + term
    x = x_ref[...] + _dot(merged.astype(BF16), wo_ref[...])
    o_ref[...] = _ffn_rows(x, g_ref, w1_ref, w2_ref, gf_ref, acc_ref, final_norm)


def _merge_ffn(x, ya, yb, yc, gates, wa, wb, wc, wo, g, w1, w2, gf, final_norm):
    m = x.shape[0]
    row = lambda width: pl.BlockSpec((TM_FFN, width), lambda i: (i, 0))
    return pl.pallas_call(
        functools.partial(_merge_ffn_kernel, final_norm=final_norm),
        grid=(m // TM_FFN,),
        in_specs=[row(D_MODEL), row(W_PAIRS), row(W_PAIRS), row(W_PAIRS), row(GG_W),
                  _resident(wa.shape), _resident(wb.shape), _resident(wc.shape), _resident(wo.shape),
                  _resident((1, D_MODEL)), _resident(w1.shape), _resident(w2.shape), _resident((1, D_MODEL))],
        out_specs=row(D_MODEL),
        out_shape=jax.ShapeDtypeStruct((m, D_MODEL), F32),
        scratch_shapes=[pltpu.VMEM((TM_FFN, D_MODEL), F32)],
        compiler_params=_cparams(("arbitrary",)),
        name="merge_ffn",
    )(x, ya, yb, yc, gates, wa, wb, wc, wo, g, w1, w2, gf)


def _pad_heads(w, axis):
    pad = [(0, 0)] * w.ndim
    pad[axis] = (0, W_PAIRS - w.shape[axis])
    return jnp.pad(w, pad)


def _layout_w_in(w_in):
    sizes = [W_FOX, W_FOX, W_FOX, N_HEADS_FOX, W_SB, W_SB, W_SB, W_DSA, KV_LATENT,
             N_IDX_HEADS * IDX_DIM, IDX_DIM, N_IDX_HEADS, 3 * D_MODEL]
    pieces, acc = [], 0
    for s in sizes:
        pieces.append(w_in[:, acc:acc + s])
        acc += s
    q_a, k_a, v_a, f_a, q_b, k_b, v_b, q_c, c_kv, q_i, k_i, w_i, gates = pieces
    scale = HEAD_DIM ** -0.5
    zeros = lambda n: jnp.zeros((D_MODEL, n), w_in.dtype)
    cols = [q_a * scale, k_a, v_a,
            _pad_heads(q_b * scale, 1), _pad_heads(k_b, 1), _pad_heads(v_b, 1),
            q_c * scale, q_i, k_i, c_kv,
            f_a, zeros(8 - N_HEADS_FOX), w_i, zeros(GS_W - 8 - N_IDX_HEADS),
            gates]
    out = jnp.concatenate(cols, axis=1).astype(BF16)
    assert out.shape == (D_MODEL, D_IN_PAD)
    return out


def _layout_ffn(w_gu, w_down):
    wg = w_gu[:, :D_FF].reshape(D_MODEL, N_FF_CHUNKS, FF_CHUNK)
    wu = w_gu[:, D_FF:].reshape(D_MODEL, N_FF_CHUNKS, FF_CHUNK)
    w1 = jnp.concatenate([wg, wu], axis=2).transpose(1, 0, 2).astype(BF16)
    w2 = w_down.reshape(N_FF_CHUNKS, FF_CHUNK, D_MODEL).astype(BF16)
    return w1, w2


def _rope_tables(seq):
    half = ROPE_DIM // 2
    pos = jnp.arange(seq, dtype=F32)
    inv = ROPE_THETA ** (-jnp.arange(0, ROPE_DIM, 2, dtype=F32) / ROPE_DIM)
    ang = pos[:, None] * inv[None, :]
    cos, sin = jnp.cos(ang), jnp.sin(ang)
    ones = jnp.ones((seq, HEAD_DIM - ROPE_DIM), F32)
    zeros_rest = jnp.zeros((seq, HEAD_DIM - ROPE_DIM), F32)
    zeros_half = jnp.zeros((seq, half), F32)
    c = jnp.concatenate([cos, cos, ones], axis=1)
    s1 = jnp.concatenate([-sin, zeros_half, zeros_rest], axis=1)
    s2 = jnp.concatenate([zeros_half, sin, zeros_rest], axis=1)
    two = lambda t: jnp.concatenate([t, t], axis=1)
    return two(c), two(s1), two(s2)


@jax.jit
def kernel(x, g_ffn1, w_ffn1_gu, w_ffn1_down, g_mix, w_in, b_forget, g_kv_latent, w_kv_up, g_idx_k,
           w_up_fox, w_up_sb, w_up_dsa, w_out, g_ffn2, w_ffn2_gu, w_ffn2_down, g_final):
    batch, seq, _ = x.shape
    depth = g_ffn1.shape[0]
    rc, rs1, rs2 = _rope_tables(seq)
    xf = x.reshape(batch * seq, D_MODEL)
    row = lambda v: v.reshape(1, -1).astype(F32)
    gf = row(g_final)
    for layer in range(depth):
        w1, w2 = _layout_ffn(w_ffn1_gu[layer], w_ffn1_down[layer])
        xf = _ffn(xf, row(g_ffn1[layer]), w1, w2, gf, False)

        gki = jnp.concatenate([jnp.ones((IDX_DIM,), F32), g_idx_k[layer]]).reshape(1, LANES)
        qk_a, qk_b, vt_a, vt_b, dq, dkk, vt, small, gates = _proj(
            xf, row(g_mix[layer]), _layout_w_in(w_in[layer]), w_kv_up[layer].astype(BF16),
            row(g_kv_latent[layer]), gki, rc, rs1, rs2, batch, seq)
        bcol = jnp.zeros((LANES, 1), F32).at[:N_HEADS_FOX, 0].set(b_forget[layer])
        c, ct, wt = _gate(small, bcol, batch, seq)
        y_a = _fox(qk_a, vt_a, c, ct, batch, seq)
        y_b = _sb(qk_b, vt_b, batch, seq)
        y_c = _dsa(dq, dkk, vt, wt, batch, seq)
        w1, w2 = _layout_ffn(w_ffn2_gu[layer], w_ffn2_down[layer])
        xf = _merge_ffn(xf, y_a, y_b, y_c, gates,
                        w_up_fox[layer].astype(BF16), _pad_heads(w_up_sb[layer], 0).astype(BF16),
                        _pad_heads(w_up_dsa[layer], 0).astype(BF16), w_out[layer].astype(BF16),
                        row(g_ffn2[layer]), w1, w2, gf, layer == depth - 1)
    return xf.reshape(batch, seq, D_MODEL)
```
